```python
import jax, jax.numpy as jnp
from jax import lax
import numpy as np

D_MODEL = 1024
BATCH = 16
SEQ = 4096
DEPTH = 1
DEC_BATCH = 32
DEC_SEQ = 32
PAST_LEN = 4096

CHUNK = 64
RET_HEADS = 4
RET_QK_DIM = 256
RET_V_DIM = D_MODEL // RET_HEADS
RET_QK_WIDTH = RET_HEADS * RET_QK_DIM
RET_WIDTH = RET_HEADS * RET_V_DIM
SB_HEADS = 8
SB_HEAD_DIM = D_MODEL // SB_HEADS
SB_WIDTH = SB_HEADS * SB_HEAD_DIM
SB_BLOCK = 128
D_FF = 4 * D_MODEL
ROPE_BASE = 10000.0
EPS = 1e-6
IN_WIDTHS = (RET_QK_WIDTH, RET_QK_WIDTH, RET_WIDTH, RET_WIDTH,
             SB_WIDTH, SB_WIDTH, SB_WIDTH, D_MODEL, D_MODEL)
IN_WIDTH = sum(IN_WIDTHS)

kernel_name = "retention_stickbreaking_parallel_streaming_step"


def rmsnorm(x, w):
    xf = x.astype(jnp.float32)
    y = xf * lax.rsqrt(jnp.mean(xf * xf, axis=-1, keepdims=True) + EPS)
    return (y * w.astype(jnp.float32)).astype(x.dtype)


def rotary(x, pos):
    half = x.shape[-1] // 2
    inv_freq = ROPE_BASE ** (-jnp.arange(half, dtype=jnp.float32) / half)
    ang = pos.astype(jnp.float32)[:, None] * inv_freq[None, :]
    cos = jnp.cos(ang)[None, :, None, :]
    sin = jnp.sin(ang)[None, :, None, :]
    xf = x.astype(jnp.float32)
    x1, x2 = xf[..., :half], xf[..., half:]
    return jnp.concatenate([x1 * cos - x2 * sin, x1 * sin + x2 * cos], axis=-1).astype(x.dtype)


def retention_log_gamma():
    return jnp.log1p(-jnp.exp2(-5.0 - jnp.arange(RET_HEADS, dtype=jnp.float32)))


def project(h, pos, w_in_l, q_norm_w, k_norm_w):
    B, T, _ = h.shape
    bounds = np.concatenate([[0], np.cumsum(IN_WIDTHS)])
    parts = [h @ w_in_l[:, int(bounds[i]):int(bounds[i + 1])] for i in range(len(IN_WIDTHS))]
    rq, rk, rv, rg, sq, sk, sv, ga, gb = parts
    rq = rotary(rq.reshape(B, T, RET_HEADS, RET_QK_DIM), pos)
    rk = rotary(rk.reshape(B, T, RET_HEADS, RET_QK_DIM), pos) * (RET_QK_DIM ** -0.5)
    rv = rv.reshape(B, T, RET_HEADS, RET_V_DIM)
    sq = rmsnorm(sq.reshape(B, T, SB_HEADS, SB_HEAD_DIM), q_norm_w)
    sk = rmsnorm(sk.reshape(B, T, SB_HEADS, SB_HEAD_DIM), k_norm_w)
    sv = sv.reshape(B, T, SB_HEADS, SB_HEAD_DIM)
    return rq, rk, rv, rg, sq, sk, sv, ga, gb


def retention_chunk(S, q, k, v, log_gamma):
    L = q.shape[1]
    idx = jnp.arange(L, dtype=jnp.float32)
    decay = jnp.exp(log_gamma[:, None, None] * jnp.abs(idx[:, None] - idx[None, :]))
    scores = jnp.einsum('bihd,bjhd->bhij', q, k) * decay[None]
    o_intra = jnp.einsum('bhij,bjhe->bihe', scores, v)
    q_decay = jnp.exp(log_gamma[None, :] * (idx[:, None] + 1.0))
    o_cross = jnp.einsum('bihd,bhde->bihe', q, S) * q_decay[None, :, :, None]
    k_decay = jnp.exp(log_gamma[None, :] * (L - 1.0 - idx[:, None]))
    S_new = (jnp.exp(log_gamma * L)[None, :, None, None] * S
             + jnp.einsum('bjhd,bjhe->bhde', k * k_decay[None, :, :, None], v))
    return S_new, o_intra + o_cross


def retention_prompt(q, k, v, log_gamma):
    B, T = q.shape[:2]
    nc = T // CHUNK
    to_chunks = lambda a: jnp.moveaxis(a.astype(jnp.float32).reshape(B, nc, CHUNK, *a.shape[2:]), 1, 0)
    S0 = jnp.zeros((B, RET_HEADS, RET_QK_DIM, RET_V_DIM), jnp.float32)
    S_fin, o = lax.scan(lambda S, xs: retention_chunk(S, xs[0], xs[1], xs[2], log_gamma),
                        S0, (to_chunks(q), to_chunks(k), to_chunks(v)))
    o = jnp.moveaxis(o, 0, 1).reshape(B, T, RET_HEADS, RET_V_DIM)
    return o, S_fin


def stick_breaking_block(q, k, v, q_pos, k_pos):
    z = jnp.einsum('bqhd,bkhd->bhqk', q, k).astype(jnp.float32) * (SB_HEAD_DIM ** -0.5)
    mask = (k_pos[None, :] < q_pos[:, None])[None, None]
    u = jnp.where(mask, jax.nn.log_sigmoid(-z), 0.0)
    after = lax.cumsum(u, axis=3, reverse=True) - u
    w = jnp.where(mask, jnp.exp(jax.nn.log_sigmoid(z) + after), 0.0)
    return jnp.einsum('bhqk,bkhd->bqhd', w.astype(v.dtype), v)


def stick_breaking_prompt(q, k, v):
    T = q.shape[1]
    pos = jnp.arange(T, dtype=jnp.int32)
    outs = []
    for b in range(T // SB_BLOCK):
        lo, hi = b * SB_BLOCK, (b + 1) * SB_BLOCK
        outs.append(stick_breaking_block(q[:, lo:hi], k[:, :hi], v[:, :hi], pos[lo:hi], pos[:hi]))
    return jnp.concatenate(outs, axis=1)


def merge(o_ret, rg, o_sb, ga, gb, ret_norm_w_l, w_out_l):
    B, T = o_ret.shape[:2]
    r = rmsnorm(o_ret, ret_norm_w_l).reshape(B, T, RET_WIDTH).astype(rg.dtype) * jax.nn.silu(rg)
    s = o_sb.reshape(B, T, SB_WIDTH)
    mix = jax.nn.sigmoid(ga) * r + jax.nn.sigmoid(gb) * s
    return mix @ w_out_l


def sq_relu_mlp(h, w_up_l, w_down_l):
    return jnp.square(jax.nn.relu(h @ w_up_l)) @ w_down_l


def setup_inputs(seed: int = 0) -> dict:
    key = jax.random.key(seed)
    ks = jax.random.split(key, 16)
    f32 = jnp.float32
    nrm = lambda k, shape, s: jax.random.normal(k, shape, f32) * s
    return {
        "x_prompt": nrm(ks[0], (BATCH, SEQ, D_MODEL), 1.0),
        "x_sample": nrm(ks[1], (DEC_BATCH, DEC_SEQ, D_MODEL), 1.0),
        "cache_sb_k": nrm(ks[2], (DEPTH, DEC_BATCH, PAST_LEN, SB_HEADS, SB_HEAD_DIM), 1.0),
        "cache_sb_v": nrm(ks[3], (DEPTH, DEC_BATCH, PAST_LEN, SB_HEADS, SB_HEAD_DIM), 1.0),
        "state_ret": nrm(ks[4], (DEPTH, DEC_BATCH, RET_HEADS, RET_QK_DIM, RET_V_DIM), 0.5),
        "norm_mix_w": 1.0 + nrm(ks[5], (DEPTH, D_MODEL), 0.01),
        "w_in": nrm(ks[6], (DEPTH, D_MODEL, IN_WIDTH), D_MODEL ** -0.5),
        "ret_norm_w": 1.0 + nrm(ks[7], (DEPTH, RET_HEADS, RET_V_DIM), 0.01),
        "sb_q_norm_w": 1.0 + nrm(ks[8], (DEPTH, SB_HEAD_DIM), 0.01),
        "sb_k_norm_w": 1.0 + nrm(ks[9], (DEPTH, SB_HEAD_DIM), 0.01),
        "w_out": nrm(ks[10], (DEPTH, D_MODEL, D_MODEL), D_MODEL ** -0.5),
        "norm_mlp_w": 1.0 + nrm(ks[11], (DEPTH, D_MODEL), 0.01),
        "w_up": nrm(ks[12], (DEPTH, D_MODEL, D_FF), D_MODEL ** -0.5),
        "w_down": nrm(ks[13], (DEPTH, D_FF, D_MODEL), D_FF ** -0.5),
    }


def reference(x_prompt, x_sample, cache_sb_k, cache_sb_v, state_ret, norm_mix_w, w_in,
              ret_norm_w, sb_q_norm_w, sb_k_norm_w, w_out, norm_mlp_w, w_up, w_down):
    log_gamma = retention_log_gamma()
    T = x_prompt.shape[1]
    L = x_sample.shape[1]
    past = cache_sb_k.shape[2]
    pos_p = jnp.arange(T, dtype=jnp.int32)
    pos_s = past + jnp.arange(L, dtype=jnp.int32)
    k_pos_s = jnp.arange(past + L, dtype=jnp.int32)
    xp, xs = x_prompt, x_sample
    kp_l, vp_l, sp_l, ks_l, vs_l, ss_l = [], [], [], [], [], []
    for l in range(DEPTH):
        hp = rmsnorm(xp, norm_mix_w[l])
        rq, rk, rv, rg, sq, sk, sv, ga, gb = project(hp, pos_p, w_in[l], sb_q_norm_w[l], sb_k_norm_w[l])
        o_ret, S_p = retention_prompt(rq, rk, rv, log_gamma)
        o_sb = stick_breaking_prompt(sq, sk, sv)
        xp = xp + merge(o_ret, rg, o_sb, ga, gb, ret_norm_w[l], w_out[l])
        xp = xp + sq_relu_mlp(rmsnorm(xp, norm_mlp_w[l]), w_up[l], w_down[l])
        kp_l.append(sk)
        vp_l.append(sv)
        sp_l.append(S_p.astype(state_ret.dtype))
        hs = rmsnorm(xs, norm_mix_w[l])
        rq, rk, rv, rg, sq, sk, sv, ga, gb = project(hs, pos_s, w_in[l], sb_q_norm_w[l], sb_k_norm_w[l])
        S_s, o_ret_s = retention_chunk(state_ret[l].astype(jnp.float32), rq.astype(jnp.float32),
                                       rk.astype(jnp.float32), rv.astype(jnp.float32), log_gamma)
        k_all = jnp.concatenate([cache_sb_k[l].astype(sk.dtype), sk], axis=1)
        v_all = jnp.concatenate([cache_sb_v[l].astype(sv.dtype), sv], axis=1)
        o_sb_s = stick_breaking_block(sq, k_all, v_all, pos_s, k_pos_s)
        xs = xs + merge(o_ret_s, rg, o_sb_s, ga, gb, ret_norm_w[l], w_out[l])
        xs = xs + sq_relu_mlp(rmsnorm(xs, norm_mlp_w[l]), w_up[l], w_down[l])
        ks_l.append(sk)
        vs_l.append(sv)
        ss_l.append(S_s.astype(state_ret.dtype))
    return (xp, xs, jnp.stack(kp_l), jnp.stack(vp_l), jnp.stack(sp_l),
            jnp.stack(ks_l), jnp.stack(vs_l), jnp.stack(ss_l))
```

```python
import functools
import math

import jax
import jax.numpy as jnp
from jax import lax
from jax.experimental import pallas as pl
from jax.experimental.pallas import tpu as pltpu

D_MODEL = 1024
CHUNK = 64
RET_HEADS = 4
RET_QK_DIM = 256
RET_V_DIM = D_MODEL // RET_HEADS
SB_HEADS = 8
SB_HEAD_DIM = D_MODEL // SB_HEADS
D_FF = 4 * D_MODEL
ROPE_BASE = 10000.0
EPS = 1e-6
N_GROUPS = 9

F32 = jnp.float32
BF16 = jnp.bfloat16

VMEM_LIMIT_BYTES = 56 * 1024 * 1024
TOKEN_TILE = 256
RET_BLOCK = 256
RET_STEP = 1024
SB_BLOCK_Q = 256
SB_BLOCK_K = 256
DECODE_KEY_TILE = 1024
DECODE_NEW_PAD = 128


def _nt_dot(a, b):
    return lax.dot_general(a, b, (((1,), (1,)), ((), ())), preferred_element_type=F32)


def _tn_dot(a, b):
    return lax.dot_general(a, b, (((0,), (0,)), ((), ())), preferred_element_type=F32)


def _dot(a, b):
    return jnp.dot(a, b, preferred_element_type=F32)


def _sigmoid(x):
    return 1.0 / (1.0 + jnp.exp(-x))


def _rms(x):
    return x * lax.rsqrt(jnp.mean(x * x, axis=-1, keepdims=True) + EPS)


def _resident(shape):
    return pl.BlockSpec(shape, lambda *_: (0,) * len(shape), pipeline_mode=pl.Buffered(1))


def _proj_kernel(x_ref, nw_ref, w_ref, cos_ref, sin_ref, qnw_ref, knw_ref,
                 rq_ref, rk_ref, rv_ref, g1_ref, sq_ref, skf_ref, skb_ref, svf_ref, svb_ref, g2_ref):
    h = (_rms(x_ref[...]) * nw_ref[...]).astype(BF16)

    def group(g):
        return _dot(h, w_ref[:, g * D_MODEL:(g + 1) * D_MODEL])

    cos = cos_ref[...]
    sin = sin_ref[...]
    half = RET_QK_DIM // 2

    def rotary(p, out_ref, scale):
        for hd in range(RET_HEADS):
            lo = hd * RET_QK_DIM
            x1 = p[:, lo:lo + half]
            x2 = p[:, lo + half:lo + 2 * half]
            o1 = x1 * cos - x2 * sin
            o2 = x1 * sin + x2 * cos
            if scale != 1.0:
                o1 = o1 * scale
                o2 = o2 * scale
            out_ref[:, lo:lo + half] = o1.astype(out_ref.dtype)
            out_ref[:, lo + half:lo + 2 * half] = o2.astype(out_ref.dtype)

    def head_norm(p, w, out_refs):
        for hd in range(SB_HEADS):
            lo = hd * SB_HEAD_DIM
            y = _rms(p[:, lo:lo + SB_HEAD_DIM]) * w
            for r in out_refs:
                r[:, lo:lo + SB_HEAD_DIM] = y.astype(r.dtype)

    rotary(group(0), rq_ref, 1.0)
    rotary(group(1), rk_ref, RET_QK_DIM ** -0.5)
    rv_ref[...] = group(2).astype(BF16)
    rg = group(3)
    head_norm(group(4), qnw_ref[...], (sq_ref,))
    head_norm(group(5), knw_ref[...], (skf_ref, skb_ref))
    sv = group(6)
    svf_ref[...] = sv
    svb_ref[...] = sv.astype(BF16)
    ga = group(7)
    g1_ref[...] = (rg * _sigmoid(rg) * _sigmoid(ga)).astype(BF16)
    g2_ref[...] = _sigmoid(group(8)).astype(BF16)


def _project(x, norm_w, w_in_bf, cos_tab, sin_tab, qnw, knw):
    n = x.shape[0]
    tm = TOKEN_TILE
    assert n % tm == 0 and cos_tab.shape[0] % tm == 0
    pos_blocks = cos_tab.shape[0] // tm
    tile = pl.BlockSpec((tm, D_MODEL), lambda i: (i, 0))
    pos_tile = pl.BlockSpec((tm, RET_QK_DIM // 2), lambda i: (i % pos_blocks, 0))
    bf = jax.ShapeDtypeStruct((n, D_MODEL), BF16)
    f32 = jax.ShapeDtypeStruct((n, D_MODEL), F32)
    return pl.pallas_call(
        _proj_kernel,
        grid=(n // tm,),
        in_specs=[tile, _resident((1, D_MODEL)), _resident((D_MODEL, N_GROUPS * D_MODEL)),
                  pos_tile, pos_tile, _resident((1, SB_HEAD_DIM)), _resident((1, SB_HEAD_DIM))],
        out_specs=[tile] * 10,
        out_shape=[bf, bf, bf, bf, bf, f32, bf, f32, bf, bf],
        compiler_params=pltpu.CompilerParams(
            dimension_semantics=("arbitrary",), vmem_limit_bytes=VMEM_LIMIT_BYTES),
        name="proj",
    )(x, norm_w, w_in_bf, cos_tab, sin_tab, qnw, knw)


def _retention_kernel(*refs, n_sub, block, has_state):
    if has_state:
        gl_ref, q_ref, k_ref, v_ref, g1_ref, dec_ref, qd_ref, kd_ref, rnw_ref, s0_ref = refs[:10]
        rest = refs[10:]
    else:
        gl_ref, q_ref, k_ref, v_ref, g1_ref, dec_ref, qd_ref, kd_ref, rnw_ref = refs[:9]
        s0_ref = None
        rest = refs[9:]
    r_ref, sout_ref, s_ref = rest
    hd = pl.program_id(1)
    t = pl.program_id(2)

    @pl.when(t == 0)
    def _():
        if has_state:
            s_ref[...] = s0_ref[...]
        else:
            s_ref[...] = jnp.zeros_like(s_ref)

    gl = gl_ref[hd]
    dec = dec_ref[...]
    qd = qd_ref[...]
    kd = kd_ref[...]
    rnw = rnw_ref[...]
    for sb in range(n_sub):
        rows = slice(sb * block, (sb + 1) * block)
        q = q_ref[rows, :]
        k = k_ref[rows, :]
        v = v_ref[rows, :]
        s_prev = s_ref[...]
        scores = _nt_dot(q, k) * dec
        o = _dot(scores.astype(BF16), v) + _dot(q, s_prev.astype(BF16)) * qd
        k_dec = (k.astype(F32) * kd).astype(BF16)
        s_ref[...] = gl * s_prev + _tn_dot(k_dec, v)
        r = _rms(o) * rnw
        r_ref[rows, :] = (r * g1_ref[rows, :].astype(F32)).astype(BF16)

    @pl.when(t == pl.num_programs(2) - 1)
    def _():
        sout_ref[...] = s_ref[...]


def _retention_tables(block):
    log_gamma = jnp.log1p(-jnp.exp2(-5.0 - jnp.arange(RET_HEADS, dtype=F32)))
    idx = jnp.arange(block, dtype=F32)
    dist = jnp.abs(idx[:, None] - idx[None, :])
    chunk = jnp.arange(block, dtype=jnp.int32) // CHUNK
    visible = chunk[None, :] <= chunk[:, None]
    dec = jnp.where(visible[None], jnp.exp(log_gamma[:, None, None] * dist[None]), 0.0)
    qd = jnp.exp(log_gamma[:, None] * (idx[None, :] + 1.0))
    kd = jnp.exp(log_gamma[:, None] * (block - 1.0 - idx[None, :]))
    widen = lambda a: jnp.broadcast_to(a[:, :, None], (RET_HEADS, block, RET_QK_DIM))
    gl = jnp.exp(log_gamma * block)
    return gl, dec, widen(qd), widen(kd)


def _retention(rq, rk, rv, g1, ret_norm_w, state0, batch, frames):
    block = min(RET_BLOCK, frames)
    step = min(RET_STEP, frames)
    assert frames % step == 0 and step % block == 0
    assert block % CHUNK == 0 or block < CHUNK
    n_t = frames // step
    gl, dec, qd, kd = _retention_tables(block)
    has_state = state0 is not None
    tile = pl.BlockSpec((step, RET_QK_DIM), lambda b, h, t: (b * n_t + t, h))
    per_head = lambda rows, cols: pl.BlockSpec((None, rows, cols), lambda b, h, t: (h, 0, 0))
    state_spec = pl.BlockSpec((None, None, RET_QK_DIM, RET_V_DIM), lambda b, h, t: (b, h, 0, 0))
    in_specs = [pl.BlockSpec(memory_space=pltpu.SMEM), tile, tile, tile, tile,
                per_head(block, block), per_head(block, RET_QK_DIM), per_head(block, RET_QK_DIM),
                per_head(1, RET_V_DIM)]
    args = [gl, rq, rk, rv, g1, dec, qd, kd, ret_norm_w.reshape(RET_HEADS, 1, RET_V_DIM)]
    if has_state:
        in_specs.append(state_spec)
        args.append(state0)
    return pl.pallas_call(
        functools.partial(_retention_kernel, n_sub=step // block, block=block, has_state=has_state),
        grid=(batch, RET_HEADS, n_t),
        in_specs=in_specs,
        out_specs=[tile, state_spec],
        out_shape=[jax.ShapeDtypeStruct(rq.shape, BF16),
                   jax.ShapeDtypeStruct((batch, RET_HEADS, RET_QK_DIM, RET_V_DIM), F32)],
        scratch_shapes=[pltpu.VMEM((RET_QK_DIM, RET_V_DIM), F32)],
        compiler_params=pltpu.CompilerParams(
            dimension_semantics=("arbitrary", "arbitrary", "arbitrary"),
            vmem_limit_bytes=VMEM_LIMIT_BYTES),
        name="retention",
    )(*args)


def _suffix_matrix(n):
    idx = jnp.arange(n)
    return (idx[:, None] > idx[None, :]).astype(BF16)


def _sb_block(z, v, tri, carry, mask):
    soft = jnp.log(1.0 + jnp.exp(-jnp.abs(z)))
    log_beta = jnp.minimum(z, 0.0) - soft
    u = log_beta - z
    if mask is not None:
        u = jnp.where(mask, u, 0.0)
    u_hi = u.astype(BF16)
    u_lo = (u - u_hi.astype(F32)).astype(BF16)
    after = _dot(u_hi, tri) + _dot(u_lo, tri) + carry
    w = jnp.exp(log_beta + after)
    if mask is not None:
        w = jnp.where(mask, w, 0.0)
    out = _dot(w.astype(BF16), v)
    return out, carry + jnp.sum(u, axis=-1, keepdims=True)


def _sb_prompt_kernel(q_ref, k_ref, v_ref, tri_ref, o_ref, *, bq, bk):
    i = pl.program_id(2)
    q = q_ref[...]
    tri = tri_ref[...]
    scale = SB_HEAD_DIM ** -0.5
    n_diag = bq // bk
    row = i * bq + lax.broadcasted_iota(jnp.int32, (bq, bk), 0)
    col = lax.broadcasted_iota(jnp.int32, (bq, bk), 1)

    acc = jnp.zeros((bq, SB_HEAD_DIM), F32)
    carry = jnp.zeros((bq, 1), F32)
    for d in reversed(range(n_diag)):
        start = pl.multiple_of(i * bq + d * bk, bk)
        kb = k_ref[pl.ds(start, bk), :]
        vb = v_ref[pl.ds(start, bk), :]
        mask = (col + (i * bq + d * bk)) < row
        out, carry = _sb_block(_nt_dot(q, kb) * scale, vb, tri, carry, mask)
        acc = acc + out

    def body(jj, state):
        acc, carry = state
        start = pl.multiple_of((i * n_diag - 1 - jj) * bk, bk)
        kb = k_ref[pl.ds(start, bk), :]
        vb = v_ref[pl.ds(start, bk), :]
        out, carry = _sb_block(_nt_dot(q, kb) * scale, vb, tri, carry, None)
        return acc + out, carry

    acc, carry = lax.fori_loop(0, i * n_diag, body, (acc, carry))
    o_ref[...] = acc.astype(o_ref.dtype)


def _sb_prompt(sq, sk, sv, batch, frames):
    bq = min(SB_BLOCK_Q, frames)
    bk = min(SB_BLOCK_K, bq)
    assert frames % bq == 0 and bq % bk == 0
    n_q = frames // bq
    q_tile = pl.BlockSpec((bq, SB_HEAD_DIM), lambda b, h, i: (b * n_q + i, h))
    kv_all = pl.BlockSpec((frames, SB_HEAD_DIM), lambda b, h, i: (b, h))
    return pl.pallas_call(
        functools.partial(_sb_prompt_kernel, bq=bq, bk=bk),
        grid=(batch, SB_HEADS, n_q),
        in_specs=[q_tile, kv_all, kv_all, _resident((bk, bk))],
        out_specs=q_tile,
        out_shape=jax.ShapeDtypeStruct(sq.shape, BF16),
        compiler_params=pltpu.CompilerParams(
            dimension_semantics=("arbitrary", "arbitrary", "arbitrary"),
            vmem_limit_bytes=VMEM_LIMIT_BYTES),
        name="sb_prompt",
    )(sq, sk, sv, _suffix_matrix(bk))


def _sb_decode_kernel(q_ref, kn_ref, vn_ref, kc_ref, vc_ref, tri_ref, o_ref,
                      qbd_ref, acc_ref, carry_ref, *, new, key_tile, bk):
    kt = pl.program_id(1)
    rows = SB_HEADS * new
    scale = SB_HEAD_DIM ** -0.5
    tri = tri_ref[...]

    @pl.when(kt == 0)
    def _():
        q = q_ref[...]
        col = lax.broadcasted_iota(jnp.int32, (new, D_MODEL), 1)
        for hd in range(SB_HEADS):
            keep = (col >= hd * SB_HEAD_DIM) & (col < (hd + 1) * SB_HEAD_DIM)
            qbd_ref[hd * new:(hd + 1) * new, :] = jnp.where(keep, q, jnp.zeros_like(q))
        qbd = qbd_ref[...]
        pad = kn_ref.shape[0]
        l_q = jnp.concatenate([lax.broadcasted_iota(jnp.int32, (new, pad), 0)] * SB_HEADS, axis=0)
        l_k = lax.broadcasted_iota(jnp.int32, (rows, pad), 1)
        out, carry = _sb_block(_nt_dot(qbd, kn_ref[...]) * scale, vn_ref[...], tri[:pad, :pad],
                               jnp.zeros((rows, 1), F32), l_k < l_q)
        acc_ref[...] = out
        carry_ref[...] = carry

    qbd = qbd_ref[...]
    carry = carry_ref[...]
    acc = acc_ref[...]
    for sb in reversed(range(key_tile // bk)):
        kb = kc_ref[sb * bk:(sb + 1) * bk, :].astype(BF16)
        vb = vc_ref[sb * bk:(sb + 1) * bk, :].astype(BF16)
        out, carry = _sb_block(_nt_dot(qbd, kb) * scale, vb, tri, carry, None)
        acc = acc + out
    acc_ref[...] = acc
    carry_ref[...] = carry

    @pl.when(kt == pl.num_programs(1) - 1)
    def _():
        for hd in range(SB_HEADS):
            cols = slice(hd * SB_HEAD_DIM, (hd + 1) * SB_HEAD_DIM)
            o_ref[:, cols] = acc_ref[hd * new:(hd + 1) * new, cols].astype(o_ref.dtype)


def _sb_decode(sq, sk, sv, cache_k, cache_v, batch, new):
    past = cache_k.shape[1]
    key_tile = min(DECODE_KEY_TILE, past)
    bk = min(SB_BLOCK_K, key_tile)
    assert past % key_tile == 0 and key_tile % bk == 0 and new <= DECODE_NEW_PAD <= bk
    n_kt = past // key_tile
    pad = DECODE_NEW_PAD
    pad_keys = lambda a: jnp.pad(a.reshape(batch, new, D_MODEL), ((0, 0), (0, pad - new), (0, 0)))
    rows = SB_HEADS * new
    q_tile = pl.BlockSpec((new, D_MODEL), lambda b, kt: (b, 0))
    new_tile = pl.BlockSpec((None, pad, D_MODEL), lambda b, kt: (b, 0, 0))
    cache_tile = pl.BlockSpec((None, key_tile, D_MODEL), lambda b, kt: (b, n_kt - 1 - kt, 0))
    return pl.pallas_call(
        functools.partial(_sb_decode_kernel, new=new, key_tile=key_tile, bk=bk),
        grid=(batch, n_kt),
        in_specs=[q_tile, new_tile, new_tile, cache_tile, cache_tile, _resident((bk, bk))],
        out_specs=q_tile,
        out_shape=jax.ShapeDtypeStruct(sq.shape, BF16),
        scratch_shapes=[pltpu.VMEM((rows, D_MODEL), BF16), pltpu.VMEM((rows, D_MODEL), F32),
                        pltpu.VMEM((rows, 1), F32)],
        compiler_params=pltpu.CompilerParams(
            dimension_semantics=("arbitrary", "arbitrary"), vmem_limit_bytes=VMEM_LIMIT_BYTES),
        name="sb_decode",
    )(sq, pad_keys(sk), pad_keys(sv), cache_k, cache_v, _suffix_matrix(bk))


def _out_mlp_kernel(x_ref, r_ref, s_ref, g2_ref, wo_ref, nw_ref, wu_ref, wd_ref, y_ref):
    mix = r_ref[...].astype(F32) + g2_ref[...].astype(F32) * s_ref[...].astype(F32)
    x1 = x_ref[...] + _dot(mix.astype(BF16), wo_ref[...])
    h = (_rms(x1) * nw_ref[...]).astype(BF16)
    up = jnp.maximum(_dot(h, wu_ref[...]), 0.0)
    y_ref[...] = x1 + _dot((up * up).astype(BF16), wd_ref[...])


def _out_mlp(x, r, s, g2, w_out_bf, norm_w, w_up_bf, w_down_bf):
    n = x.shape[0]
    tm = TOKEN_TILE
    assert n % tm == 0
    tile = pl.BlockSpec((tm, D_MODEL), lambda i: (i, 0))
    return pl.pallas_call(
        _out_mlp_kernel,
        grid=(n // tm,),
        in_specs=[tile, tile, tile, tile, _resident((D_MODEL, D_MODEL)), _resident((1, D_MODEL)),
                  _resident((D_MODEL, D_FF)), _resident((D_FF, D_MODEL))],
        out_specs=tile,
        out_shape=jax.ShapeDtypeStruct((n, D_MODEL), F32),
        compiler_params=pltpu.CompilerParams(
            dimension_semantics=("arbitrary",), vmem_limit_bytes=VMEM_LIMIT_BYTES),
        name="out_mlp",
    )(x, r, s, g2, w_out_bf, norm_w, w_up_bf, w_down_bf)


def _rope_tables(pos):
    half = RET_QK_DIM // 2
    inv_freq = ROPE_BASE ** (-jnp.arange(half, dtype=F32) / half)
    ang = pos.astype(F32)[:, None] * inv_freq[None, :]
    return jnp.cos(ang), jnp.sin(ang)


def kernel(x_prompt, x_sample, cache_sb_k, cache_sb_v, state_ret, norm_mix_w, w_in, ret_norm_w,
           sb_q_norm_w, sb_k_norm_w, w_out, norm_mlp_w, w_up, w_down):
    depth = w_in.shape[0]
    batch, frames, _ = x_prompt.shape
    dec_batch, new, _ = x_sample.shape
    past = cache_sb_k.shape[2]

    cos_p, sin_p = _rope_tables(jnp.arange(frames, dtype=jnp.int32))
    cos_s, sin_s = _rope_tables(past + jnp.arange(new, dtype=jnp.int32))
    reps = max(TOKEN_TILE // new, 1)
    cos_s, sin_s = jnp.tile(cos_s, (reps, 1)), jnp.tile(sin_s, (reps, 1))

    xp = x_prompt.reshape(batch * frames, D_MODEL)
    xs = x_sample.reshape(dec_batch * new, D_MODEL)
    outs = [[] for _ in range(6)]
    for l in range(depth):
        w_in_bf = w_in[l].astype(BF16)
        w_out_bf = w_out[l].astype(BF16)
        w_up_bf = w_up[l].astype(BF16)
        w_down_bf = w_down[l].astype(BF16)
        nmix = norm_mix_w[l].reshape(1, D_MODEL)
        nmlp = norm_mlp_w[l].reshape(1, D_MODEL)
        qnw = sb_q_norm_w[l].reshape(1, SB_HEAD_DIM)
        knw = sb_k_norm_w[l].reshape(1, SB_HEAD_DIM)

        rq, rk, rv, g1, sq, sk, skb, sv, svb, g2 = _project(xp, nmix, w_in_bf, cos_p, sin_p, qnw, knw)
        r, state_p = _retention(rq, rk, rv, g1, ret_norm_w[l], None, batch, frames)
        o_sb = _sb_prompt(sq, skb, svb, batch, frames)
        xp = _out_mlp(xp, r, o_sb, g2, w_out_bf, nmlp, w_up_bf, w_down_bf)
        outs[0].append(sk.reshape(batch, frames, SB_HEADS, SB_HEAD_DIM))
        outs[1].append(sv.reshape(batch, frames, SB_HEADS, SB_HEAD_DIM))
        outs[2].append(state_p.astype(state_ret.dtype))

        rq, rk, rv, g1, sq, sk, skb, sv, svb, g2 = _project(xs, nmix, w_in_bf, cos_s, sin_s, qnw, knw)
        r, state_s = _retention(rq, rk, rv, g1, ret_norm_w[l], state_ret[l].astype(F32), dec_batch, new)
        o_sb = _sb_decode(sq, skb, svb,
                          cache_sb_k[l].reshape(dec_batch, past, D_MODEL),
                          cache_sb_v[l].reshape(dec_batch, past, D_MODEL), dec_batch, new)
        xs = _out_mlp(xs, r, o_sb, g2, w_out_bf, nmlp, w_up_bf, w_down_bf)
        outs[3].append(sk.reshape(dec_batch, new, SB_HEADS, SB_HEAD_DIM))
        outs[4].append(sv.reshape(dec_batch, new, SB_HEADS, SB_HEAD_DIM))
        outs[5].append(state_s.astype(state_ret.dtype))

    return (xp.reshape(batch, frames, D_MODEL), xs.reshape(dec_batch, new, D_MODEL),
            jnp.stack(outs[0]), jnp.stack(outs[1]), jnp.stack(outs[2]),
            jnp.stack(outs[3]), jnp.stack(outs[4]), jnp.stack(outs[5]))
```

```python
import functools
import math

import jax
import jax.numpy as jnp
from jax import lax
from jax.experimental import pallas as pl
from jax.experimental.pallas import tpu as pltpu

D_MODEL = 1024
CHUNK = 64
RET_HEADS = 4
RET_QK_DIM = 256
RET_V_DIM = D_MODEL // RET_HEADS
SB_HEADS = 8
SB_HEAD_DIM = D_MODEL // SB_HEADS
D_FF = 4 * D_MODEL
ROPE_BASE = 10000.0
EPS = 1e-6
LOG2_E = math.log2(math.e)
N_GROUPS = 9

F32 = jnp.float32
BF16 = jnp.bfloat16

VMEM_LIMIT_BYTES = 56 * 1024 * 1024
TOKEN_TILE = 256
RET_BLOCK = 256
RET_STEP = 1024
SB_BLOCK_Q = 256
SB_BLOCK_K = 256
SB_HEADS_PER_STEP = 4
DECODE_KEY_TILE = 1024
DECODE_NEW_PAD = 128


def _nt_dot(a, b):
    return lax.dot_general(a, b, (((1,), (1,)), ((), ())), preferred_element_type=F32)


def _tn_dot(a, b):
    return lax.dot_general(a, b, (((0,), (0,)), ((), ())), preferred_element_type=F32)


def _dot(a, b):
    return jnp.dot(a, b, preferred_element_type=F32)


def _sigmoid(x):
    return 1.0 / (1.0 + jnp.exp(-x))


def _rms(x):
    return x * lax.rsqrt(jnp.mean(x * x, axis=-1, keepdims=True) + EPS)


def _resident(shape):
    return pl.BlockSpec(shape, lambda *_: (0,) * len(shape), pipeline_mode=pl.Buffered(1))


def _proj_kernel(x_ref, nw_ref, w_ref, cos_ref, sin_ref, qnw_ref, knw_ref,
                 rq_ref, rk_ref, rv_ref, g1_ref, sq_ref, skf_ref, skb_ref, svf_ref, svb_ref, g2_ref):
    h = (_rms(x_ref[...]) * nw_ref[...]).astype(BF16)

    def group(g):
        return _dot(h, w_ref[:, g * D_MODEL:(g + 1) * D_MODEL])

    cos = cos_ref[...]
    sin = sin_ref[...]
    half = RET_QK_DIM // 2

    def rotary(p, out_ref, scale):
        for hd in range(RET_HEADS):
            lo = hd * RET_QK_DIM
            x1 = p[:, lo:lo + half]
            x2 = p[:, lo + half:lo + 2 * half]
            o1 = x1 * cos - x2 * sin
            o2 = x1 * sin + x2 * cos
            if scale != 1.0:
                o1 = o1 * scale
                o2 = o2 * scale
            out_ref[:, lo:lo + half] = o1.astype(out_ref.dtype)
            out_ref[:, lo + half:lo + 2 * half] = o2.astype(out_ref.dtype)

    def head_norm(p, w, out_refs):
        for hd in range(SB_HEADS):
            lo = hd * SB_HEAD_DIM
            y = _rms(p[:, lo:lo + SB_HEAD_DIM]) * w
            for r in out_refs:
                r[:, lo:lo + SB_HEAD_DIM] = y.astype(r.dtype)

    rotary(group(0), rq_ref, 1.0)
    rotary(group(1), rk_ref, RET_QK_DIM ** -0.5)
    rv_ref[...] = group(2).astype(BF16)
    rg = group(3)
    head_norm(group(4), qnw_ref[...], (sq_ref,))
    head_norm(group(5), knw_ref[...], (skf_ref, skb_ref))
    sv = group(6)
    svf_ref[...] = sv
    svb_ref[...] = sv.astype(BF16)
    ga = group(7)
    g1_ref[...] = (rg * _sigmoid(rg) * _sigmoid(ga)).astype(BF16)
    g2_ref[...] = _sigmoid(group(8)).astype(BF16)


def _project(x, norm_w, w_in_bf, cos_tab, sin_tab, qnw, knw):
    n = x.shape[0]
    tm = TOKEN_TILE
    assert n % tm == 0 and cos_tab.shape[0] % tm == 0
    pos_blocks = cos_tab.shape[0] // tm
    tile = pl.BlockSpec((tm, D_MODEL), lambda i: (i, 0))
    pos_tile = pl.BlockSpec((tm, RET_QK_DIM // 2), lambda i: (i % pos_blocks, 0))
    bf = jax.ShapeDtypeStruct((n, D_MODEL), BF16)
    f32 = jax.ShapeDtypeStruct((n, D_MODEL), F32)
    return pl.pallas_call(
        _proj_kernel,
        grid=(n // tm,),
        in_specs=[tile, _resident((1, D_MODEL)), _resident((D_MODEL, N_GROUPS * D_MODEL)),
                  pos_tile, pos_tile, _resident((1, SB_HEAD_DIM)), _resident((1, SB_HEAD_DIM))],
        out_specs=[tile] * 10,
        out_shape=[bf, bf, bf, bf, bf, f32, bf, f32, bf, bf],
        compiler_params=pltpu.CompilerParams(
            dimension_semantics=("arbitrary",), vmem_limit_bytes=VMEM_LIMIT_BYTES),
        name="proj",
    )(x, norm_w, w_in_bf, cos_tab, sin_tab, qnw, knw)


def _retention_kernel(*refs, n_sub, block, has_state):
    if has_state:
        gl_ref, q_ref, k_ref, v_ref, g1_ref, dec_ref, qd_ref, kd_ref, rnw_ref, s0_ref = refs[:10]
        rest = refs[10:]
    else:
        gl_ref, q_ref, k_ref, v_ref, g1_ref, dec_ref, qd_ref, kd_ref, rnw_ref = refs[:9]
        s0_ref = None
        rest = refs[9:]
    r_ref, sout_ref, s_ref = rest
    hd = pl.program_id(1)
    t = pl.program_id(2)

    @pl.when(t == 0)
    def _():
        if has_state:
            s_ref[...] = s0_ref[...]
        else:
            s_ref[...] = jnp.zeros_like(s_ref)

    gl = gl_ref[hd]
    dec = dec_ref[...]
    qd = qd_ref[...]
    kd = kd_ref[...]
    rnw = rnw_ref[...]
    for sb in range(n_sub):
        rows = slice(sb * block, (sb + 1) * block)
        q = q_ref[rows, :]
        k = k_ref[rows, :]
        v = v_ref[rows, :]
        s_prev = s_ref[...]
        scores = _nt_dot(q, k) * dec
        o = _dot(scores.astype(BF16), v) + _dot(q, s_prev.astype(BF16)) * qd
        k_dec = (k.astype(F32) * kd).astype(BF16)
        s_ref[...] = gl * s_prev + _tn_dot(k_dec, v)
        r = _rms(o) * rnw
        r_ref[rows, :] = (r * g1_ref[rows, :].astype(F32)).astype(BF16)

    @pl.when(t == pl.num_programs(2) - 1)
    def _():
        sout_ref[...] = s_ref[...]


def _retention_tables(block):
    log_gamma = jnp.log1p(-jnp.exp2(-5.0 - jnp.arange(RET_HEADS, dtype=F32)))
    idx = jnp.arange(block, dtype=F32)
    dist = jnp.abs(idx[:, None] - idx[None, :])
    chunk = jnp.arange(block, dtype=jnp.int32) // CHUNK
    visible = chunk[None, :] <= chunk[:, None]
    dec = jnp.where(visible[None], jnp.exp(log_gamma[:, None, None] * dist[None]), 0.0)
    qd = jnp.exp(log_gamma[:, None] * (idx[None, :] + 1.0))
    kd = jnp.exp(log_gamma[:, None] * (block - 1.0 - idx[None, :]))
    widen = lambda a: jnp.broadcast_to(a[:, :, None], (RET_HEADS, block, RET_QK_DIM))
    gl = jnp.exp(log_gamma * block)
    return gl, dec, widen(qd), widen(kd)


def _retention(rq, rk, rv, g1, ret_norm_w, state0, batch, frames):
    block = min(RET_BLOCK, frames)
    step = min(RET_STEP, frames)
    assert frames % step == 0 and step % block == 0
    assert block % CHUNK == 0 or block < CHUNK
    n_t = frames // step
    gl, dec, qd, kd = _retention_tables(block)
    has_state = state0 is not None
    tile = pl.BlockSpec((step, RET_QK_DIM), lambda b, h, t: (b * n_t + t, h))
    per_head = lambda rows, cols: pl.BlockSpec((None, rows, cols), lambda b, h, t: (h, 0, 0))
    state_spec = pl.BlockSpec((None, None, RET_QK_DIM, RET_V_DIM), lambda b, h, t: (b, h, 0, 0))
    in_specs = [pl.BlockSpec(memory_space=pltpu.SMEM), tile, tile, tile, tile,
                per_head(block, block), per_head(block, RET_QK_DIM), per_head(block, RET_QK_DIM),
                per_head(1, RET_V_DIM)]
    args = [gl, rq, rk, rv, g1, dec, qd, kd, ret_norm_w.reshape(RET_HEADS, 1, RET_V_DIM)]
    if has_state:
        in_specs.append(state_spec)
        args.append(state0)
    return pl.pallas_call(
        functools.partial(_retention_kernel, n_sub=step // block, block=block, has_state=has_state),
        grid=(batch, RET_HEADS, n_t),
        in_specs=in_specs,
        out_specs=[tile, state_spec],
        out_shape=[jax.ShapeDtypeStruct(rq.shape, BF16),
                   jax.ShapeDtypeStruct((batch, RET_HEADS, RET_QK_DIM, RET_V_DIM), F32)],
        scratch_shapes=[pltpu.VMEM((RET_QK_DIM, RET_V_DIM), F32)],
        compiler_params=pltpu.CompilerParams(
            dimension_semantics=("arbitrary", "arbitrary", "arbitrary"),
            vmem_limit_bytes=VMEM_LIMIT_BYTES),
        name="retention",
    )(*args)


def _suffix_matrix(n):
    idx = jnp.arange(n)
    m = (idx[:, None] > idx[None, :]).astype(BF16)
    return jnp.concatenate([m, m], axis=0)


def _sb_gates(s, mask):
    z = s * (SB_HEAD_DIM ** -0.5 * LOG2_E)
    neg_abs = pltpu.bitcast(pltpu.bitcast(z, jnp.uint32) | jnp.uint32(0x80000000), F32)
    soft = jnp.log2(1.0 + jnp.exp2(neg_abs))
    log_beta = jnp.minimum(z, 0.0) - soft
    u = log_beta - z
    if mask is not None:
        u = jnp.where(mask, u, 0.0)
    u_hi = u.astype(BF16)
    u_lo = (u - u_hi.astype(F32)).astype(BF16)
    return log_beta, jnp.concatenate([u_hi, u_lo], axis=1), jnp.sum(u, axis=-1, keepdims=True)


def _sb_weights(log_beta, after, carry, mask):
    w = jnp.exp2(log_beta + carry + after)
    if mask is not None:
        w = jnp.where(mask, w, 0.0)
    return w.astype(BF16)


def _sb_blocks(qs, ks, vs, tri2, carries, mask):
    n = len(qs)
    s = [_nt_dot(qs[i], ks[i]) for i in range(n)]
    gates = [_sb_gates(s[i], mask) for i in range(n)]
    after = [_dot(gates[i][1], tri2) for i in range(n)]
    w = [_sb_weights(gates[i][0], after[i], carries[i], mask) for i in range(n)]
    outs = [_dot(w[i], vs[i]) for i in range(n)]
    return outs, [carries[i] + gates[i][2] for i in range(n)]


def _sb_key_run(q, ks, vs, tri2, carry):
    n = len(ks)
    s = [_nt_dot(q, ks[i]) for i in range(n)]
    gates = [_sb_gates(s[i], None) for i in range(n)]
    after = [_dot(gates[i][1], tri2) for i in range(n)]
    out = None
    for i in range(n):
        o = _dot(_sb_weights(gates[i][0], after[i], carry, None), vs[i])
        out = o if out is None else out + o
        carry = carry + gates[i][2]
    return out, carry


def _sb_prompt_kernel(q_ref, k_ref, v_ref, tri_ref, o_ref, acc_ref, carry_ref, *, bq, bk, heads):
    i = pl.program_id(2)
    tri2 = tri_ref[...]
    n_diag = bq // bk
    row = lax.broadcasted_iota(jnp.int32, (bq, bk), 0)
    col = lax.broadcasted_iota(jnp.int32, (bq, bk), 1)
    acc_ref[...] = jnp.zeros_like(acc_ref)
    carry_ref[...] = jnp.zeros_like(carry_ref)

    head_cols = [slice(hd * SB_HEAD_DIM, (hd + 1) * SB_HEAD_DIM) for hd in range(heads)]

    def step(start, mask):
        outs, carries = _sb_blocks(
            [q_ref[:, c] for c in head_cols],
            [k_ref[pl.ds(start, bk), c] for c in head_cols],
            [v_ref[pl.ds(start, bk), c] for c in head_cols],
            tri2, [carry_ref[hd] for hd in range(heads)], mask)
        for hd, c in enumerate(head_cols):
            acc_ref[:, c] += outs[hd]
            carry_ref[hd] = carries[hd]

    for d in reversed(range(n_diag)):
        step(pl.multiple_of(i * bq + d * bk, bk), (col + d * bk) < row)

    def body(jj, _):
        step(pl.multiple_of((i * n_diag - 1 - jj) * bk, bk), None)
        return 0

    lax.fori_loop(0, i * n_diag, body, 0)
    o_ref[...] = acc_ref[...].astype(o_ref.dtype)


def _sb_prompt(sq, sk, sv, batch, frames):
    bq = min(SB_BLOCK_Q, frames)
    bk = min(SB_BLOCK_K, bq)
    heads = SB_HEADS_PER_STEP
    assert frames % bq == 0 and bq % bk == 0 and SB_HEADS % heads == 0
    n_q = frames // bq
    width = heads * SB_HEAD_DIM
    q_tile = pl.BlockSpec((bq, width), lambda b, h, i: (b * n_q + i, h))
    kv_all = pl.BlockSpec((frames, width), lambda b, h, i: (b, h))
    return pl.pallas_call(
        functools.partial(_sb_prompt_kernel, bq=bq, bk=bk, heads=heads),
        grid=(batch, SB_HEADS // heads, n_q),
        in_specs=[q_tile, kv_all, kv_all, _resident((2 * bk, bk))],
        out_specs=q_tile,
        out_shape=jax.ShapeDtypeStruct(sq.shape, BF16),
        scratch_shapes=[pltpu.VMEM((bq, width), F32), pltpu.VMEM((heads, bq, 1), F32)],
        compiler_params=pltpu.CompilerParams(
            dimension_semantics=("arbitrary", "arbitrary", "arbitrary"),
            vmem_limit_bytes=VMEM_LIMIT_BYTES),
        name="sb_prompt",
    )(sq, sk, sv, _suffix_matrix(bk))


def _sb_decode_kernel(q_ref, kn_ref, vn_ref, kc_ref, vc_ref, tri_ref, o_ref,
                      qbd_ref, acc_ref, carry_ref, *, new, key_tile, bk):
    kt = pl.program_id(1)
    rows = SB_HEADS * new
    tri2 = tri_ref[...]

    @pl.when(kt == 0)
    def _():
        q = q_ref[...]
        col = lax.broadcasted_iota(jnp.int32, (new, D_MODEL), 1)
        for hd in range(SB_HEADS):
            keep = (col >= hd * SB_HEAD_DIM) & (col < (hd + 1) * SB_HEAD_DIM)
            qbd_ref[hd * new:(hd + 1) * new, :] = jnp.where(keep, q, jnp.zeros_like(q))
        pad = kn_ref.shape[0]
        l_q = jnp.concatenate([lax.broadcasted_iota(jnp.int32, (new, pad), 0)] * SB_HEADS, axis=0)
        l_k = lax.broadcasted_iota(jnp.int32, (rows, pad), 1)
        tri2_pad = jnp.concatenate([tri2[:pad, :pad], tri2[bk:bk + pad, :pad]], axis=0)
        outs, carries = _sb_blocks([qbd_ref[...]], [kn_ref[...]], [vn_ref[...]], tri2_pad,
                                   [jnp.zeros((rows, 1), F32)], l_k < l_q)
        acc_ref[...] = outs[0]
        carry_ref[...] = carries[0]

    def head_major(ref, sb):
        base = sb * bk * SB_HEADS
        return jnp.concatenate(
            [ref[pl.ds(base + hd, bk, stride=SB_HEADS), :].astype(BF16) for hd in range(SB_HEADS)], axis=1)

    order = list(reversed(range(key_tile // bk)))
    out, carry = _sb_key_run(qbd_ref[...], [head_major(kc_ref, sb) for sb in order],
                             [head_major(vc_ref, sb) for sb in order], tri2, carry_ref[...])
    acc_ref[...] += out
    carry_ref[...] = carry

    @pl.when(kt == pl.num_programs(1) - 1)
    def _():
        for hd in range(SB_HEADS):
            cols = slice(hd * SB_HEAD_DIM, (hd + 1) * SB_HEAD_DIM)
            o_ref[:, cols] = acc_ref[hd * new:(hd + 1) * new, cols].astype(o_ref.dtype)


def _sb_decode(sq, sk, sv, cache_k, cache_v, batch, new):
    past = cache_k.shape[1] // SB_HEADS
    key_tile = min(DECODE_KEY_TILE, past)
    bk = min(SB_BLOCK_K, key_tile)
    assert past % key_tile == 0 and key_tile % bk == 0 and new <= DECODE_NEW_PAD <= bk
    n_kt = past // key_tile
    pad = DECODE_NEW_PAD
    pad_keys = lambda a: jnp.pad(a.reshape(batch, new, D_MODEL), ((0, 0), (0, pad - new), (0, 0)))
    rows = SB_HEADS * new
    q_tile = pl.BlockSpec((new, D_MODEL), lambda b, kt: (b, 0))
    new_tile = pl.BlockSpec((None, pad, D_MODEL), lambda b, kt: (b, 0, 0))
    cache_tile = pl.BlockSpec((None, key_tile * SB_HEADS, SB_HEAD_DIM),
                              lambda b, kt: (b, n_kt - 1 - kt, 0))
    return pl.pallas_call(
        functools.partial(_sb_decode_kernel, new=new, key_tile=key_tile, bk=bk),
        grid=(batch, n_kt),
        in_specs=[q_tile, new_tile, new_tile, cache_tile, cache_tile, _resident((2 * bk, bk))],
        out_specs=q_tile,
        out_shape=jax.ShapeDtypeStruct(sq.shape, BF16),
        scratch_shapes=[pltpu.VMEM((rows, D_MODEL), BF16), pltpu.VMEM((rows, D_MODEL), F32),
                        pltpu.VMEM((rows, 1), F32)],
        compiler_params=pltpu.CompilerParams(
            dimension_semantics=("arbitrary", "arbitrary"), vmem_limit_bytes=VMEM_LIMIT_BYTES),
        name="sb_decode",
    )(sq, pad_keys(sk), pad_keys(sv), cache_k, cache_v, _suffix_matrix(bk))


def _out_mlp_kernel(x_ref, r_ref, s_ref, g2_ref, wo_ref, nw_ref, wu_ref, wd_ref, y_ref):
    mix = r_ref[...].astype(F32) + g2_ref[...].astype(F32) * s_ref[...].astype(F32)
    x1 = x_ref[...] + _dot(mix.astype(BF16), wo_ref[...])
    h = (_rms(x1) * nw_ref[...]).astype(BF16)
    up = jnp.maximum(_dot(h, wu_ref[...]), 0.0)
    y_ref[...] = x1 + _dot((up * up).astype(BF16), wd_ref[...])


def _out_mlp(x, r, s, g2, w_out_bf, norm_w, w_up_bf, w_down_bf):
    n = x.shape[0]
    tm = TOKEN_TILE
    assert n % tm == 0
    tile = pl.BlockSpec((tm, D_MODEL), lambda i: (i, 0))
    return pl.pallas_call(
        _out_mlp_kernel,
        grid=(n // tm,),
        in_specs=[tile, tile, tile, tile, _resident((D_MODEL, D_MODEL)), _resident((1, D_MODEL)),
                  _resident((D_MODEL, D_FF)), _resident((D_FF, D_MODEL))],
        out_specs=tile,
        out_shape=jax.ShapeDtypeStruct((n, D_MODEL), F32),
        compiler_params=pltpu.CompilerParams(
            dimension_semantics=("arbitrary",), vmem_limit_bytes=VMEM_LIMIT_BYTES),
        name="out_mlp",
    )(x, r, s, g2, w_out_bf, norm_w, w_up_bf, w_down_bf)


def _rope_tables(pos):
    half = RET_QK_DIM // 2
    inv_freq = ROPE_BASE ** (-jnp.arange(half, dtype=F32) / half)
    ang = pos.astype(F32)[:, None] * inv_freq[None, :]
    return jnp.cos(ang), jnp.sin(ang)


def kernel(x_prompt, x_sample, cache_sb_k, cache_sb_v, state_ret, norm_mix_w, w_in, ret_norm_w,
           sb_q_norm_w, sb_k_norm_w, w_out, norm_mlp_w, w_up, w_down):
    depth = w_in.shape[0]
    batch, frames, _ = x_prompt.shape
    dec_batch, new, _ = x_sample.shape
    past = cache_sb_k.shape[2]

    cos_p, sin_p = _rope_tables(jnp.arange(frames, dtype=jnp.int32))
    cos_s, sin_s = _rope_tables(past + jnp.arange(new, dtype=jnp.int32))
    reps = max(TOKEN_TILE // new, 1)
    cos_s, sin_s = jnp.tile(cos_s, (reps, 1)), jnp.tile(sin_s, (reps, 1))

    xp = x_prompt.reshape(batch * frames, D_MODEL)
    xs = x_sample.reshape(dec_batch * new, D_MODEL)
    outs = [[] for _ in range(6)]
    for l in range(depth):
        w_in_bf = w_in[l].astype(BF16)
        w_out_bf = w_out[l].astype(BF16)
        w_up_bf = w_up[l].astype(BF16)
        w_down_bf = w_down[l].astype(BF16)
        nmix = norm_mix_w[l].reshape(1, D_MODEL)
        nmlp = norm_mlp_w[l].reshape(1, D_MODEL)
        qnw = sb_q_norm_w[l].reshape(1, SB_HEAD_DIM)
        knw = sb_k_norm_w[l].reshape(1, SB_HEAD_DIM)

        rq, rk, rv, g1, sq, sk, skb, sv, svb, g2 = _project(xp, nmix, w_in_bf, cos_p, sin_p, qnw, knw)
        r, state_p = _retention(rq, rk, rv, g1, ret_norm_w[l], None, batch, frames)
        o_sb = _sb_prompt(sq, skb, svb, batch, frames)
        xp = _out_mlp(xp, r, o_sb, g2, w_out_bf, nmlp, w_up_bf, w_down_bf)
        outs[0].append(sk.reshape(batch, frames, SB_HEADS, SB_HEAD_DIM))
        outs[1].append(sv.reshape(batch, frames, SB_HEADS, SB_HEAD_DIM))
        outs[2].append(state_p.astype(state_ret.dtype))

        rq, rk, rv, g1, sq, sk, skb, sv, svb, g2 = _project(xs, nmix, w_in_bf, cos_s, sin_s, qnw, knw)
        r, state_s = _retention(rq, rk, rv, g1, ret_norm_w[l], state_ret[l].astype(F32), dec_batch, new)
        o_sb = _sb_decode(sq, skb, svb,
                          cache_sb_k[l].reshape(dec_batch, past * SB_HEADS, SB_HEAD_DIM),
                          cache_sb_v[l].reshape(dec_batch, past * SB_HEADS, SB_HEAD_DIM), dec_batch, new)
        xs = _out_mlp(xs, r, o_sb, g2, w_out_bf, nmlp, w_up_bf, w_down_bf)
        outs[3].append(sk.reshape(dec_batch, new, SB_HEADS, SB_HEAD_DIM))
        outs[4].append(sv.reshape(dec_batch, new, SB_HEADS, SB_HEAD_DIM))
        outs[5].append(state_s.astype(state_ret.dtype))

    return (xp.reshape(batch, frames, D_MODEL), xs.reshape(dec_batch, new, D_MODEL),
            jnp.stack(outs[0]), jnp.stack(outs[1]), jnp.stack(outs[2]),
            jnp.stack(outs[3]), jnp.stack(outs[4]), jnp.stack(outs[5]))
```

```python
import functools
import math

import jax
import jax.numpy as jnp
from jax import lax
from jax.experimental import pallas as pl
from jax.experimental.pallas import tpu as pltpu

D_MODEL = 1024
CHUNK = 64
RET_HEADS = 4
RET_QK_DIM = 256
RET_V_DIM = D_MODEL // RET_HEADS
SB_HEADS = 8
SB_HEAD_DIM = D_MODEL // SB_HEADS
D_FF = 4 * D_MODEL
ROPE_BASE = 10000.0
EPS = 1e-6
LOG2_E = math.log2(math.e)
N_GROUPS = 9

F32 = jnp.float32
BF16 = jnp.bfloat16

VMEM_LIMIT_BYTES = 56 * 1024 * 1024
TOKEN_TILE = 256
RET_BLOCK = 256
RET_STEP = 1024
SB_BLOCK = 256
SB_HEADS_PER_STEP = 4
SB_KEY_BLOCKS_PER_TRIP = 2
SUBLANES = 8
SB_PROMPT_QUERY_SCALE = SB_HEAD_DIM ** -0.5 * 0.5
SB_DECODE_QUERY_SCALE = SB_HEAD_DIM ** -0.5 * LOG2_E
MASKED_LOG_BETA = -1e30
DECODE_KEY_TILE = 1024
DECODE_NEW_PAD = 128


def _nt_dot(a, b):
    return lax.dot_general(a, b, (((1,), (1,)), ((), ())), preferred_element_type=F32)


def _tn_dot(a, b):
    return lax.dot_general(a, b, (((0,), (0,)), ((), ())), preferred_element_type=F32)


def _dot(a, b):
    return jnp.dot(a, b, preferred_element_type=F32)


def _sigmoid(x):
    return 1.0 / (1.0 + jnp.exp(-x))


def _rms(x):
    return x * lax.rsqrt(jnp.mean(x * x, axis=-1, keepdims=True) + EPS)


def _resident(shape):
    return pl.BlockSpec(shape, lambda *_: (0,) * len(shape), pipeline_mode=pl.Buffered(1))


def _proj_kernel(*refs, key_major):
    if key_major:
        (x_ref, nw_ref, w_ref, cos_ref, sin_ref, qnw_ref, knw_ref, perm_ref, perm_t_ref,
         rq_ref, rk_ref, rv_ref, g1_ref, sq_ref, skf_ref, skb_ref, svf_ref, svb_ref, g2_ref) = refs
    else:
        (x_ref, nw_ref, w_ref, cos_ref, sin_ref, qnw_ref, knw_ref,
         rq_ref, rk_ref, rv_ref, g1_ref, sq_ref, skf_ref, skb_ref, svf_ref, svb_ref, g2_ref) = refs
    h = (_rms(x_ref[...]) * nw_ref[...]).astype(BF16)

    def group(g):
        return _dot(h, w_ref[:, g * D_MODEL:(g + 1) * D_MODEL])

    cos = cos_ref[...]
    sin = sin_ref[...]
    half = RET_QK_DIM // 2

    def rotary(p, out_ref, scale):
        for hd in range(RET_HEADS):
            lo = hd * RET_QK_DIM
            x1 = p[:, lo:lo + half]
            x2 = p[:, lo + half:lo + 2 * half]
            o1 = x1 * cos - x2 * sin
            o2 = x1 * sin + x2 * cos
            if scale != 1.0:
                o1 = o1 * scale
                o2 = o2 * scale
            out_ref[:, lo:lo + half] = o1.astype(out_ref.dtype)
            out_ref[:, lo + half:lo + 2 * half] = o2.astype(out_ref.dtype)

    def head_norm(p, w, scale=1.0):
        ys = []
        for hd in range(SB_HEADS):
            y = _rms(p[:, hd * SB_HEAD_DIM:(hd + 1) * SB_HEAD_DIM]) * w
            ys.append(y if scale == 1.0 else y * scale)
        return jnp.concatenate(ys, axis=1)

    rotary(group(0), rq_ref, 1.0)
    rotary(group(1), rk_ref, RET_QK_DIM ** -0.5)
    rv_ref[...] = group(2).astype(BF16)
    rg = group(3)
    query_scale = SB_PROMPT_QUERY_SCALE if key_major else SB_DECODE_QUERY_SCALE
    sq_ref[...] = head_norm(group(4), qnw_ref[...], query_scale).astype(BF16)
    sk = head_norm(group(5), knw_ref[...])
    skf_ref[...] = sk
    sv = group(6)
    svf_ref[...] = sv
    if key_major:
        skb_ref[...] = _dot(perm_ref[...], sk.astype(BF16)).astype(BF16)
        svb_ref[...] = _tn_dot(sv.astype(BF16), perm_t_ref[...]).astype(BF16)
    else:
        skb_ref[...] = sk.astype(BF16)
        svb_ref[...] = sv.astype(BF16)
    ga = group(7)
    g1_ref[...] = (rg * _sigmoid(rg) * _sigmoid(ga)).astype(BF16)
    g2_ref[...] = _sigmoid(group(8)).astype(BF16)


def _key_order(n):
    row = jnp.arange(n)
    key = (row % SUBLANES) * (n // SUBLANES) + row // SUBLANES
    return (key[:, None] == jnp.arange(n)[None, :]).astype(BF16)


def _project(x, norm_w, w_in_bf, cos_tab, sin_tab, qnw, knw, key_major):
    n = x.shape[0]
    tm = TOKEN_TILE
    assert n % tm == 0 and cos_tab.shape[0] % tm == 0
    pos_blocks = cos_tab.shape[0] // tm
    tile = pl.BlockSpec((tm, D_MODEL), lambda i: (i, 0))
    pos_tile = pl.BlockSpec((tm, RET_QK_DIM // 2), lambda i: (i % pos_blocks, 0))
    bf = jax.ShapeDtypeStruct((n, D_MODEL), BF16)
    f32 = jax.ShapeDtypeStruct((n, D_MODEL), F32)
    in_specs = [tile, _resident((1, D_MODEL)), _resident((D_MODEL, N_GROUPS * D_MODEL)),
                pos_tile, pos_tile, _resident((1, SB_HEAD_DIM)), _resident((1, SB_HEAD_DIM))]
    args = [x, norm_w, w_in_bf, cos_tab, sin_tab, qnw, knw]
    out_specs = [tile] * 10
    out_shape = [bf, bf, bf, bf, bf, f32, bf, f32, bf, bf]
    if key_major:
        assert tm == SB_BLOCK
        perm = _key_order(tm)
        in_specs += [_resident((tm, tm)), _resident((tm, tm))]
        args += [perm, (perm.T.astype(F32) * _value_scale(tm)[None, :]).astype(BF16)]
        out_specs[8] = pl.BlockSpec((None, D_MODEL, tm), lambda i: (i, 0, 0))
        out_shape[8] = jax.ShapeDtypeStruct((n // tm, D_MODEL, tm), BF16)
    return pl.pallas_call(
        functools.partial(_proj_kernel, key_major=key_major),
        grid=(n // tm,),
        in_specs=in_specs,
        out_specs=out_specs,
        out_shape=out_shape,
        compiler_params=pltpu.CompilerParams(
            dimension_semantics=("arbitrary",), vmem_limit_bytes=VMEM_LIMIT_BYTES),
        name="proj",
    )(*args)


def _retention_kernel(*refs, n_sub, block, has_state):
    if has_state:
        gl_ref, q_ref, k_ref, v_ref, g1_ref, dec_ref, qd_ref, kd_ref, rnw_ref, s0_ref = refs[:10]
        rest = refs[10:]
    else:
        gl_ref, q_ref, k_ref, v_ref, g1_ref, dec_ref, qd_ref, kd_ref, rnw_ref = refs[:9]
        s0_ref = None
        rest = refs[9:]
    r_ref, sout_ref, s_ref = rest
    hd = pl.program_id(1)
    t = pl.program_id(2)

    @pl.when(t == 0)
    def _():
        if has_state:
            s_ref[...] = s0_ref[...]
        else:
            s_ref[...] = jnp.zeros_like(s_ref)

    gl = gl_ref[hd]
    dec = dec_ref[...]
    qd = qd_ref[...]
    kd = kd_ref[...]
    rnw = rnw_ref[...]
    for sb in range(n_sub):
        rows = slice(sb * block, (sb + 1) * block)
        q = q_ref[rows, :]
        k = k_ref[rows, :]
        v = v_ref[rows, :]
        s_prev = s_ref[...]
        scores = _nt_dot(q, k) * dec
        o = _dot(scores.astype(BF16), v) + _dot(q, s_prev.astype(BF16)) * qd
        k_dec = (k.astype(F32) * kd).astype(BF16)
        s_ref[...] = gl * s_prev + _tn_dot(k_dec, v)
        r = _rms(o) * rnw
        r_ref[rows, :] = (r * g1_ref[rows, :].astype(F32)).astype(BF16)

    @pl.when(t == pl.num_programs(2) - 1)
    def _():
        sout_ref[...] = s_ref[...]


def _retention_tables(block):
    log_gamma = jnp.log1p(-jnp.exp2(-5.0 - jnp.arange(RET_HEADS, dtype=F32)))
    idx = jnp.arange(block, dtype=F32)
    dist = jnp.abs(idx[:, None] - idx[None, :])
    chunk = jnp.arange(block, dtype=jnp.int32) // CHUNK
    visible = chunk[None, :] <= chunk[:, None]
    dec = jnp.where(visible[None], jnp.exp(log_gamma[:, None, None] * dist[None]), 0.0)
    qd = jnp.exp(log_gamma[:, None] * (idx[None, :] + 1.0))
    kd = jnp.exp(log_gamma[:, None] * (block - 1.0 - idx[None, :]))
    widen = lambda a: jnp.broadcast_to(a[:, :, None], (RET_HEADS, block, RET_QK_DIM))
    gl = jnp.exp(log_gamma * block)
    return gl, dec, widen(qd), widen(kd)


def _retention(rq, rk, rv, g1, ret_norm_w, state0, batch, frames):
    block = min(RET_BLOCK, frames)
    step = min(RET_STEP, frames)
    assert frames % step == 0 and step % block == 0
    assert block % CHUNK == 0 or block < CHUNK
    n_t = frames // step
    gl, dec, qd, kd = _retention_tables(block)
    has_state = state0 is not None
    tile = pl.BlockSpec((step, RET_QK_DIM), lambda b, h, t: (b * n_t + t, h))
    per_head = lambda rows, cols: pl.BlockSpec((None, rows, cols), lambda b, h, t: (h, 0, 0))
    state_spec = pl.BlockSpec((None, None, RET_QK_DIM, RET_V_DIM), lambda b, h, t: (b, h, 0, 0))
    in_specs = [pl.BlockSpec(memory_space=pltpu.SMEM), tile, tile, tile, tile,
                per_head(block, block), per_head(block, RET_QK_DIM), per_head(block, RET_QK_DIM),
                per_head(1, RET_V_DIM)]
    args = [gl, rq, rk, rv, g1, dec, qd, kd, ret_norm_w.reshape(RET_HEADS, 1, RET_V_DIM)]
    if has_state:
        in_specs.append(state_spec)
        args.append(state0)
    return pl.pallas_call(
        functools.partial(_retention_kernel, n_sub=step // block, block=block, has_state=has_state),
        grid=(batch, RET_HEADS, n_t),
        in_specs=in_specs,
        out_specs=[tile, state_spec],
        out_shape=[jax.ShapeDtypeStruct(rq.shape, BF16),
                   jax.ShapeDtypeStruct((batch, RET_HEADS, RET_QK_DIM, RET_V_DIM), F32)],
        scratch_shapes=[pltpu.VMEM((RET_QK_DIM, RET_V_DIM), F32)],
        compiler_params=pltpu.CompilerParams(
            dimension_semantics=("arbitrary", "arbitrary", "arbitrary"),
            vmem_limit_bytes=VMEM_LIMIT_BYTES),
        name="retention",
    )(*args)


def _suffix_matrix(n):
    idx = jnp.arange(n)
    m = (idx[:, None] > idx[None, :]).astype(BF16)
    return jnp.concatenate([m, m], axis=0)


def _sb_gates(z, mask):
    neg_abs = pltpu.bitcast(pltpu.bitcast(z, jnp.uint32) | jnp.uint32(0x80000000), F32)
    soft = jnp.log2(1.0 + jnp.exp2(neg_abs))
    log_beta = jnp.minimum(z, 0.0) - soft
    u = log_beta - z
    if mask is not None:
        u = jnp.where(mask, u, 0.0)
        log_beta = jnp.where(mask, log_beta, MASKED_LOG_BETA)
    u_hi = u.astype(BF16)
    u_lo = (u - u_hi.astype(F32)).astype(BF16)
    return log_beta, jnp.concatenate([u_hi, u_lo], axis=1), jnp.sum(u, axis=-1, keepdims=True)


def _sb_weights(log_beta, after, carry):
    return jnp.exp2(log_beta + carry + after).astype(BF16)


def _sb_key_run(q, ks, vs, tri2, carry, mask):
    n = len(ks)
    z = [_nt_dot(q, ks[i]) for i in range(n)]
    gates = [_sb_gates(z[i], mask) for i in range(n)]
    after = [_dot(gates[i][1], tri2) for i in range(n)]
    out = None
    for i in range(n):
        o = _dot(_sb_weights(gates[i][0], after[i], carry), vs[i])
        out = o if out is None else out + o
        carry = carry + gates[i][2]
    return out, carry


def _sb_doubled_gates(half_z, visible):
    t = jnp.tanh(half_z)
    beta2 = 1.0 + t
    rest2 = 1.0 - t
    if visible is not None:
        beta2 = jnp.where(visible, beta2, 0.0)
        rest2 = jnp.where(visible, rest2, 2.0)
    return beta2, rest2


def _sb_run_scan(half_z, visible):
    groups = half_z.shape[0] // SUBLANES
    pieces = [None] * groups
    run = None
    for r in reversed(range(groups)):
        rows = slice(r * SUBLANES, (r + 1) * SUBLANES)
        beta2, rest2 = _sb_doubled_gates(half_z[rows], None if visible is None else visible[rows])
        pieces[r] = beta2 if run is None else beta2 * run
        run = rest2 if run is None else run * rest2
    return pieces, run * 2.0 ** -groups


def _value_scale(n):
    groups = n // SUBLANES
    return jnp.exp2((jnp.arange(n) // SUBLANES - groups).astype(F32))


def _suffix_over_sublanes(t):
    s = lax.broadcasted_iota(jnp.int32, t.shape, 0)
    y = t
    for k in (1, 2, 4):
        below = pltpu.roll(y, SUBLANES - k, axis=0)
        y = y * jnp.where(s + k < SUBLANES, below, 1.0)
    exclusive = jnp.where(s + 1 < SUBLANES, pltpu.roll(y, SUBLANES - 1, axis=0), 1.0)
    return exclusive, jnp.broadcast_to(y[0:1, :], t.shape)


def _sb_units(units, carries, visible):
    carries = dict(carries)
    outs = {}
    z, scans = {}, {}

    def stage(s, u):
        hd, q, k, v_t = units[u]
        if s == 0:
            z[u] = _nt_dot(k, q)
        elif s == 1:
            scans[u] = _sb_run_scan(z.pop(u), visible)
        else:
            pieces, totals = scans.pop(u)
            later_runs, total = _suffix_over_sublanes(totals)
            factor = later_runs * carries[hd]
            w = jnp.concatenate([p * factor for p in pieces], axis=0).astype(BF16)
            o = _dot(v_t, w)
            outs[hd] = o if hd not in outs else outs[hd] + o
            carries[hd] = carries[hd] * total

    n_stages = 3
    for t in range(len(units) + n_stages - 1):
        for s in reversed(range(n_stages)):
            if 0 <= t - s < len(units):
                stage(s, t - s)
    return outs, carries


def _sb_prompt_kernel(q_ref, k_ref, vt_ref, o_ref, acc_ref, carry_ref, *, blk, heads, blocks_per_trip):
    i = pl.program_id(2)
    head_cols = [slice(hd * SB_HEAD_DIM, (hd + 1) * SB_HEAD_DIM) for hd in range(heads)]

    def units(blocks):
        out = []
        for block in blocks:
            start = pl.multiple_of(block * blk, blk)
            out += [(hd, q_ref[:, c], k_ref[pl.ds(start, blk), c], vt_ref[block, c, :])
                    for hd, c in enumerate(head_cols)]
        return out

    def run(blocks, visible, first=False):
        carries = {hd: (jnp.ones((SUBLANES, blk), F32) if first else carry_ref[hd])
                   for hd in range(heads)}
        outs, carries = _sb_units(units(blocks), carries, visible)
        for hd in range(heads):
            if first:
                acc_ref[hd] = outs[hd]
            else:
                acc_ref[hd] += outs[hd]
            carry_ref[hd] = carries[hd]

    row = lax.broadcasted_iota(jnp.int32, (blk, blk), 0)
    key = (row & (SUBLANES - 1)) * (blk // SUBLANES) + (row >> (SUBLANES.bit_length() - 1))
    run([i], key < lax.broadcasted_iota(jnp.int32, (blk, blk), 1), first=True)

    for p in [1 << b for b in range(blocks_per_trip.bit_length() - 1)]:
        @pl.when((i & p) != 0)
        def _(p=p):
            top = i - 1 - (i & (p - 1))
            run([top - d for d in range(p)], None)

    def trip(t, _):
        top = i - 1 - (i & (blocks_per_trip - 1)) - blocks_per_trip * t
        run([top - d for d in range(blocks_per_trip)], None)
        return 0

    lax.fori_loop(0, i // blocks_per_trip, trip, 0)
    for hd, c in enumerate(head_cols):
        o_ref[:, c] = acc_ref[hd].T.astype(o_ref.dtype)


def _sb_prompt(sq, sk_km, sv_t, batch, frames):
    blk = SB_BLOCK
    heads = SB_HEADS_PER_STEP
    assert frames % blk == 0 and SB_HEADS % heads == 0
    n_q = frames // blk
    width = heads * SB_HEAD_DIM
    q_tile = pl.BlockSpec((blk, width), lambda b, h, i: (b * n_q + i, h))
    k_all = pl.BlockSpec((frames, width), lambda b, h, i: (b, h))
    vt_all = pl.BlockSpec((None, n_q, width, blk), lambda b, h, i: (b, 0, h, 0))
    return pl.pallas_call(
        functools.partial(_sb_prompt_kernel, blk=blk, heads=heads, blocks_per_trip=SB_KEY_BLOCKS_PER_TRIP),
        grid=(batch, SB_HEADS // heads, n_q),
        in_specs=[q_tile, k_all, vt_all],
        out_specs=q_tile,
        out_shape=jax.ShapeDtypeStruct(sq.shape, BF16),
        scratch_shapes=[pltpu.VMEM((heads, SB_HEAD_DIM, blk), F32),
                        pltpu.VMEM((heads, SUBLANES, blk), F32)],
        compiler_params=pltpu.CompilerParams(
            dimension_semantics=("arbitrary", "arbitrary", "arbitrary"),
            vmem_limit_bytes=VMEM_LIMIT_BYTES),
        name="sb_prompt",
    )(sq, sk_km, sv_t.reshape(batch, n_q, D_MODEL, blk))


def _sb_decode_kernel(q_ref, kn_ref, vn_ref, kc_ref, vc_ref, tri_ref, o_ref,
                      qbd_ref, acc_ref, carry_ref, *, new, key_tile, bk):
    kt = pl.program_id(1)
    rows = SB_HEADS * new
    tri2 = tri_ref[...]

    @pl.when(kt == 0)
    def _():
        q = q_ref[...]
        col = lax.broadcasted_iota(jnp.int32, (new, D_MODEL), 1)
        for hd in range(SB_HEADS):
            keep = (col >= hd * SB_HEAD_DIM) & (col < (hd + 1) * SB_HEAD_DIM)
            qbd_ref[hd * new:(hd + 1) * new, :] = jnp.where(keep, q, jnp.zeros_like(q))
        pad = kn_ref.shape[0]
        l_q = jnp.concatenate([lax.broadcasted_iota(jnp.int32, (new, pad), 0)] * SB_HEADS, axis=0)
        l_k = lax.broadcasted_iota(jnp.int32, (rows, pad), 1)
        tri2_pad = jnp.concatenate([tri2[:pad, :pad], tri2[bk:bk + pad, :pad]], axis=0)
        out, carry = _sb_key_run(qbd_ref[...], [kn_ref[...]], [vn_ref[...]], tri2_pad,
                                 jnp.zeros((rows, 1), F32), l_k < l_q)
        acc_ref[...] = out
        carry_ref[...] = carry

    def head_major(ref, sb):
        base = sb * bk * SB_HEADS
        return jnp.concatenate(
            [ref[pl.ds(base + hd, bk, stride=SB_HEADS), :].astype(BF16) for hd in range(SB_HEADS)], axis=1)

    order = list(reversed(range(key_tile // bk)))
    out, carry = _sb_key_run(qbd_ref[...], [head_major(kc_ref, sb) for sb in order],
                             [head_major(vc_ref, sb) for sb in order], tri2, carry_ref[...], None)
    acc_ref[...] += out
    carry_ref[...] = carry

    @pl.when(kt == pl.num_programs(1) - 1)
    def _():
        for hd in range(SB_HEADS):
            cols = slice(hd * SB_HEAD_DIM, (hd + 1) * SB_HEAD_DIM)
            o_ref[:, cols] = acc_ref[hd * new:(hd + 1) * new, cols].astype(o_ref.dtype)


def _sb_decode(sq, sk, sv, cache_k, cache_v, batch, new):
    past = cache_k.shape[1] // SB_HEADS
    key_tile = min(DECODE_KEY_TILE, past)
    bk = min(SB_BLOCK, key_tile)
    assert past % key_tile == 0 and key_tile % bk == 0 and new <= DECODE_NEW_PAD <= bk
    n_kt = past // key_tile
    pad = DECODE_NEW_PAD
    pad_keys = lambda a: jnp.pad(a.reshape(batch, new, D_MODEL), ((0, 0), (0, pad - new), (0, 0)))
    rows = SB_HEADS * new
    q_tile = pl.BlockSpec((new, D_MODEL), lambda b, kt: (b, 0))
    new_tile = pl.BlockSpec((None, pad, D_MODEL), lambda b, kt: (b, 0, 0))
    cache_tile = pl.BlockSpec((None, key_tile * SB_HEADS, SB_HEAD_DIM),
                              lambda b, kt: (b, n_kt - 1 - kt, 0))
    return pl.pallas_call(
        functools.partial(_sb_decode_kernel, new=new, key_tile=key_tile, bk=bk),
        grid=(batch, n_kt),
        in_specs=[q_tile, new_tile, new_tile, cache_tile, cache_tile, _resident((2 * bk, bk))],
        out_specs=q_tile,
        out_shape=jax.ShapeDtypeStruct(sq.shape, BF16),
        scratch_shapes=[pltpu.VMEM((rows, D_MODEL), BF16), pltpu.VMEM((rows, D_MODEL), F32),
                        pltpu.VMEM((rows, 1), F32)],
        compiler_params=pltpu.CompilerParams(
            dimension_semantics=("arbitrary", "arbitrary"), vmem_limit_bytes=VMEM_LIMIT_BYTES),
        name="sb_decode",
    )(sq, pad_keys(sk), pad_keys(sv), cache_k, cache_v, _suffix_matrix(bk))


def _out_mlp_kernel(x_ref, r_ref, s_ref, g2_ref, wo_ref, nw_ref, wu_ref, wd_ref, y_ref):
    mix = r_ref[...].astype(F32) + g2_ref[...].astype(F32) * s_ref[...].astype(F32)
    x1 = x_ref[...] + _dot(mix.astype(BF16), wo_ref[...])
    h = (_rms(x1) * nw_ref[...]).astype(BF16)
    up = jnp.maximum(_dot(h, wu_ref[...]), 0.0)
    y_ref[...] = x1 + _dot((up * up).astype(BF16), wd_ref[...])


def _out_mlp(x, r, s, g2, w_out_bf, norm_w, w_up_bf, w_down_bf):
    n = x.shape[0]
    tm = TOKEN_TILE
    assert n % tm == 0
    tile = pl.BlockSpec((tm, D_MODEL), lambda i: (i, 0))
    return pl.pallas_call(
        _out_mlp_kernel,
        grid=(n // tm,),
        in_specs=[tile, tile, tile, tile, _resident((D_MODEL, D_MODEL)), _resident((1, D_MODEL)),
                  _resident((D_MODEL, D_FF)), _resident((D_FF, D_MODEL))],
        out_specs=tile,
        out_shape=jax.ShapeDtypeStruct((n, D_MODEL), F32),
        compiler_params=pltpu.CompilerParams(
            dimension_semantics=("arbitrary",), vmem_limit_bytes=VMEM_LIMIT_BYTES),
        name="out_mlp",
    )(x, r, s, g2, w_out_bf, norm_w, w_up_bf, w_down_bf)


def _rope_tables(pos):
    half = RET_QK_DIM // 2
    inv_freq = ROPE_BASE ** (-jnp.arange(half, dtype=F32) / half)
    ang = pos.astype(F32)[:, None] * inv_freq[None, :]
    return jnp.cos(ang), jnp.sin(ang)


def kernel(x_prompt, x_sample, cache_sb_k, cache_sb_v, state_ret, norm_mix_w, w_in, ret_norm_w,
           sb_q_norm_w, sb_k_norm_w, w_out, norm_mlp_w, w_up, w_down):
    depth = w_in.shape[0]
    batch, frames, _ = x_prompt.shape
    dec_batch, new, _ = x_sample.shape
    past = cache_sb_k.shape[2]

    cos_p, sin_p = _rope_tables(jnp.arange(frames, dtype=jnp.int32))
    cos_s, sin_s = _rope_tables(past + jnp.arange(new, dtype=jnp.int32))
    reps = max(TOKEN_TILE // new, 1)
    cos_s, sin_s = jnp.tile(cos_s, (reps, 1)), jnp.tile(sin_s, (reps, 1))

    xp = x_prompt.reshape(batch * frames, D_MODEL)
    xs = x_sample.reshape(dec_batch * new, D_MODEL)
    outs = [[] for _ in range(6)]
    for l in range(depth):
        w_in_bf = w_in[l].astype(BF16)
        w_out_bf = w_out[l].astype(BF16)
        w_up_bf = w_up[l].astype(BF16)
        w_down_bf = w_down[l].astype(BF16)
        nmix = norm_mix_w[l].reshape(1, D_MODEL)
        nmlp = norm_mlp_w[l].reshape(1, D_MODEL)
        qnw = sb_q_norm_w[l].reshape(1, SB_HEAD_DIM)
        knw = sb_k_norm_w[l].reshape(1, SB_HEAD_DIM)

        rq, rk, rv, g1, sq, sk, skb, sv, svb, g2 = _project(xp, nmix, w_in_bf, cos_p, sin_p, qnw, knw, True)
        r, state_p = _retention(rq, rk, rv, g1, ret_norm_w[l], None, batch, frames)
        o_sb = _sb_prompt(sq, skb, svb, batch, frames)
        xp = _out_mlp(xp, r, o_sb, g2, w_out_bf, nmlp, w_up_bf, w_down_bf)
        outs[0].append(sk.reshape(batch, frames, SB_HEADS, SB_HEAD_DIM))
        outs[1].append(sv.reshape(batch, frames, SB_HEADS, SB_HEAD_DIM))
        outs[2].append(state_p.astype(state_ret.dtype))

        rq, rk, rv, g1, sq, sk, skb, sv, svb, g2 = _project(xs, nmix, w_in_bf, cos_s, sin_s, qnw, knw, False)
        r, state_s = _retention(rq, rk, rv, g1, ret_norm_w[l], state_ret[l].astype(F32), dec_batch, new)
        o_sb = _sb_decode(sq, skb, svb,
                          cache_sb_k[l].reshape(dec_batch, past * SB_HEADS, SB_HEAD_DIM),
                          cache_sb_v[l].reshape(dec_batch, past * SB_HEADS, SB_HEAD_DIM), dec_batch, new)
        xs = _out_mlp(xs, r, o_sb, g2, w_out_bf, nmlp, w_up_bf, w_down_bf)
        outs[3].append(sk.reshape(dec_batch, new, SB_HEADS, SB_HEAD_DIM))
        outs[4].append(sv.reshape(dec_batch, new, SB_HEADS, SB_HEAD_DIM))
        outs[5].append(state_s.astype(state_ret.dtype))

    return (xp.reshape(batch, frames, D_MODEL), xs.reshape(dec_batch, new, D_MODEL),
            jnp.stack(outs[0]), jnp.stack(outs[1]), jnp.stack(outs[2]),
            jnp.stack(outs[3]), jnp.stack(outs[4]), jnp.stack(outs[5]))
```

```python
import functools
import math

import jax
import jax.numpy as jnp
from jax import lax
from jax.experimental import pallas as pl
from jax.experimental.pallas import tpu as pltpu

D_MODEL = 1024
CHUNK = 64
RET_HEADS = 4
RET_QK_DIM = 256
RET_V_DIM = D_MODEL // RET_HEADS
SB_HEADS = 8
SB_HEAD_DIM = D_MODEL // SB_HEADS
D_FF = 4 * D_MODEL
ROPE_BASE = 10000.0
EPS = 1e-6
LOG2_E = math.log2(math.e)
N_GROUPS = 9

F32 = jnp.float32
BF16 = jnp.bfloat16

VMEM_LIMIT_BYTES = 56 * 1024 * 1024
TOKEN_TILE = 256
RET_BLOCK = 256
RET_STEP = 1024
RET_HEADS_PER_STEP = 2
SB_BLOCK = 256
LOGITS_AHEAD = 2
SUBLANES = 8
SB_PROMPT_QUERY_SCALE = SB_HEAD_DIM ** -0.5 * 0.5
SB_DECODE_QUERY_SCALE = SB_HEAD_DIM ** -0.5 * LOG2_E
MASKED_LOG_BETA = -1e30
DECODE_KEY_TILE = 1024
DECODE_NEW_PAD = 128


def _nt_dot(a, b):
    return lax.dot_general(a, b, (((1,), (1,)), ((), ())), preferred_element_type=F32)


def _tn_dot(a, b):
    return lax.dot_general(a, b, (((0,), (0,)), ((), ())), preferred_element_type=F32)


def _dot(a, b):
    return jnp.dot(a, b, preferred_element_type=F32)


def _sigmoid(x):
    return 1.0 / (1.0 + jnp.exp(-x))


def _rms(x):
    return x * lax.rsqrt(jnp.mean(x * x, axis=-1, keepdims=True) + EPS)


def _resident(shape):
    return pl.BlockSpec(shape, lambda *_: (0,) * len(shape), pipeline_mode=pl.Buffered(1))


def _proj_kernel(*refs, key_major):
    if key_major:
        (x_ref, nw_ref, w_ref, cos_ref, sin_ref, qnw_ref, knw_ref, perm_ref, perm_t_ref,
         rq_ref, rk_ref, rv_ref, g1_ref, sq_ref, skf_ref, skb_ref, svf_ref, svb_ref, g2_ref) = refs
    else:
        (x_ref, nw_ref, w_ref, cos_ref, sin_ref, qnw_ref, knw_ref,
         rq_ref, rk_ref, rv_ref, g1_ref, sq_ref, skf_ref, skb_ref, svf_ref, svb_ref, g2_ref) = refs
    h = (_rms(x_ref[...]) * nw_ref[...]).astype(BF16)

    def group(g):
        return _dot(h, w_ref[:, g * D_MODEL:(g + 1) * D_MODEL])

    cos = cos_ref[...]
    sin = sin_ref[...]
    half = RET_QK_DIM // 2

    def rotary(p, out_ref, scale):
        for hd in range(RET_HEADS):
            lo = hd * RET_QK_DIM
            x1 = p[:, lo:lo + half]
            x2 = p[:, lo + half:lo + 2 * half]
            o1 = x1 * cos - x2 * sin
            o2 = x1 * sin + x2 * cos
            if scale != 1.0:
                o1 = o1 * scale
                o2 = o2 * scale
            out_ref[:, lo:lo + half] = o1.astype(out_ref.dtype)
            out_ref[:, lo + half:lo + 2 * half] = o2.astype(out_ref.dtype)

    def head_norm(p, w, scale=1.0):
        ys = []
        for hd in range(SB_HEADS):
            y = _rms(p[:, hd * SB_HEAD_DIM:(hd + 1) * SB_HEAD_DIM]) * w
            ys.append(y if scale == 1.0 else y * scale)
        return jnp.concatenate(ys, axis=1)

    rotary(group(0), rq_ref, 1.0)
    rotary(group(1), rk_ref, RET_QK_DIM ** -0.5)
    rv_ref[...] = group(2).astype(BF16)
    rg = group(3)
    query_scale = SB_PROMPT_QUERY_SCALE if key_major else SB_DECODE_QUERY_SCALE
    sq_ref[...] = head_norm(group(4), qnw_ref[...], query_scale).astype(BF16)
    sk = head_norm(group(5), knw_ref[...])
    skf_ref[...] = sk
    sv = group(6)
    svf_ref[...] = sv
    if key_major:
        skb_ref[...] = _dot(perm_ref[...], sk.astype(BF16)).astype(BF16)
        svb_ref[...] = _tn_dot(sv.astype(BF16), perm_t_ref[...]).astype(BF16)
    else:
        skb_ref[...] = sk.astype(BF16)
        svb_ref[...] = sv.astype(BF16)
    ga = group(7)
    g1_ref[...] = (rg * _sigmoid(rg) * _sigmoid(ga)).astype(BF16)
    g2_ref[...] = _sigmoid(group(8)).astype(BF16)


def _key_order(n):
    row = jnp.arange(n)
    key = (row % SUBLANES) * (n // SUBLANES) + row // SUBLANES
    return (key[:, None] == jnp.arange(n)[None, :]).astype(BF16)


def _project(x, norm_w, w_in_bf, cos_tab, sin_tab, qnw, knw, key_major):
    n = x.shape[0]
    tm = TOKEN_TILE
    assert n % tm == 0 and cos_tab.shape[0] % tm == 0
    pos_blocks = cos_tab.shape[0] // tm
    tile = pl.BlockSpec((tm, D_MODEL), lambda i: (i, 0))
    pos_tile = pl.BlockSpec((tm, RET_QK_DIM // 2), lambda i: (i % pos_blocks, 0))
    bf = jax.ShapeDtypeStruct((n, D_MODEL), BF16)
    f32 = jax.ShapeDtypeStruct((n, D_MODEL), F32)
    in_specs = [tile, _resident((1, D_MODEL)), _resident((D_MODEL, N_GROUPS * D_MODEL)),
                pos_tile, pos_tile, _resident((1, SB_HEAD_DIM)), _resident((1, SB_HEAD_DIM))]
    args = [x, norm_w, w_in_bf, cos_tab, sin_tab, qnw, knw]
    out_specs = [tile] * 10
    out_shape = [bf, bf, bf, bf, bf, f32, bf, f32, bf, bf]
    if key_major:
        assert tm == SB_BLOCK
        perm = _key_order(tm)
        in_specs += [_resident((tm, tm)), _resident((tm, tm))]
        args += [perm, (perm.T.astype(F32) * _value_scale(tm)[None, :]).astype(BF16)]
        out_specs[8] = pl.BlockSpec((None, D_MODEL, tm), lambda i: (i, 0, 0))
        out_shape[8] = jax.ShapeDtypeStruct((n // tm, D_MODEL, tm), BF16)
    return pl.pallas_call(
        functools.partial(_proj_kernel, key_major=key_major),
        grid=(n // tm,),
        in_specs=in_specs,
        out_specs=out_specs,
        out_shape=out_shape,
        compiler_params=pltpu.CompilerParams(
            dimension_semantics=("arbitrary",), vmem_limit_bytes=VMEM_LIMIT_BYTES),
        name="proj",
    )(*args)


def _retention_kernel(*refs, n_sub, block, heads, has_state):
    if has_state:
        gl_ref, q_ref, k_ref, v_ref, g1_ref, dec_ref, qd_ref, kd_ref, rnw_ref, s0_ref = refs[:10]
        rest = refs[10:]
    else:
        gl_ref, q_ref, k_ref, v_ref, g1_ref, dec_ref, qd_ref, kd_ref, rnw_ref = refs[:9]
        s0_ref = None
        rest = refs[9:]
    r_ref, sout_ref, s_ref = rest
    first_head = pl.program_id(1) * heads
    t = pl.program_id(2)

    @pl.when(t == 0)
    def _():
        if has_state:
            s_ref[...] = s0_ref[...]
        else:
            s_ref[...] = jnp.zeros_like(s_ref)

    hs = range(heads)
    cols = [slice(j * RET_QK_DIM, (j + 1) * RET_QK_DIM) for j in hs]
    for sb in range(n_sub):
        rows = slice(sb * block, (sb + 1) * block)
        q = [q_ref[rows, c] for c in cols]
        k = [k_ref[rows, c] for c in cols]
        v = [v_ref[rows, c] for c in cols]
        s_prev = [s_ref[j] for j in hs]
        scores = [(_nt_dot(q[j], k[j]) * dec_ref[j]).astype(BF16) for j in hs]
        cross = [_dot(q[j], s_prev[j].astype(BF16)) * qd_ref[j] for j in hs]
        k_dec = [(k[j].astype(F32) * kd_ref[j]).astype(BF16) for j in hs]
        for j in hs:
            s_ref[j] = gl_ref[first_head + j] * s_prev[j] + _tn_dot(k_dec[j], v[j])
        for j in hs:
            r = _rms(_dot(scores[j], v[j]) + cross[j]) * rnw_ref[j]
            r_ref[rows, cols[j]] = (r * g1_ref[rows, cols[j]].astype(F32)).astype(BF16)

    @pl.when(t == pl.num_programs(2) - 1)
    def _():
        sout_ref[...] = s_ref[...]


def _retention_tables(block):
    log_gamma = jnp.log1p(-jnp.exp2(-5.0 - jnp.arange(RET_HEADS, dtype=F32)))
    idx = jnp.arange(block, dtype=F32)
    dist = jnp.abs(idx[:, None] - idx[None, :])
    chunk = jnp.arange(block, dtype=jnp.int32) // CHUNK
    visible = chunk[None, :] <= chunk[:, None]
    dec = jnp.where(visible[None], jnp.exp(log_gamma[:, None, None] * dist[None]), 0.0)
    qd = jnp.exp(log_gamma[:, None] * (idx[None, :] + 1.0))
    kd = jnp.exp(log_gamma[:, None] * (block - 1.0 - idx[None, :]))
    widen = lambda a: jnp.broadcast_to(a[:, :, None], (RET_HEADS, block, RET_QK_DIM))
    gl = jnp.exp(log_gamma * block)
    return gl, dec, widen(qd), widen(kd)


def _retention(rq, rk, rv, g1, ret_norm_w, state0, batch, frames):
    block = min(RET_BLOCK, frames)
    step = min(RET_STEP, frames)
    assert frames % step == 0 and step % block == 0
    assert block % CHUNK == 0 or block < CHUNK
    n_t = frames // step
    gl, dec, qd, kd = _retention_tables(block)
    has_state = state0 is not None
    heads = RET_HEADS_PER_STEP
    assert RET_HEADS % heads == 0
    tile = pl.BlockSpec((step, heads * RET_QK_DIM), lambda b, h, t: (b * n_t + t, h))
    per_head = lambda rows, cols: pl.BlockSpec((heads, rows, cols), lambda b, h, t: (h, 0, 0))
    state_spec = pl.BlockSpec((None, heads, RET_QK_DIM, RET_V_DIM), lambda b, h, t: (b, h, 0, 0))
    in_specs = [pl.BlockSpec(memory_space=pltpu.SMEM), tile, tile, tile, tile,
                per_head(block, block), per_head(block, RET_QK_DIM), per_head(block, RET_QK_DIM),
                per_head(1, RET_V_DIM)]
    args = [gl, rq, rk, rv, g1, dec, qd, kd, ret_norm_w.reshape(RET_HEADS, 1, RET_V_DIM)]
    if has_state:
        in_specs.append(state_spec)
        args.append(state0)
    return pl.pallas_call(
        functools.partial(_retention_kernel, n_sub=step // block, block=block, heads=heads,
                          has_state=has_state),
        grid=(batch, RET_HEADS // heads, n_t),
        in_specs=in_specs,
        out_specs=[tile, state_spec],
        out_shape=[jax.ShapeDtypeStruct(rq.shape, BF16),
                   jax.ShapeDtypeStruct((batch, RET_HEADS, RET_QK_DIM, RET_V_DIM), F32)],
        scratch_shapes=[pltpu.VMEM((heads, RET_QK_DIM, RET_V_DIM), F32)],
        compiler_params=pltpu.CompilerParams(
            dimension_semantics=("arbitrary", "arbitrary", "arbitrary"),
            vmem_limit_bytes=VMEM_LIMIT_BYTES),
        name="retention",
    )(*args)


def _suffix_matrix(n):
    idx = jnp.arange(n)
    m = (idx[:, None] > idx[None, :]).astype(BF16)
    return jnp.concatenate([m, m], axis=0)


def _sb_gates(z, mask):
    neg_abs = pltpu.bitcast(pltpu.bitcast(z, jnp.uint32) | jnp.uint32(0x80000000), F32)
    soft = jnp.log2(1.0 + jnp.exp2(neg_abs))
    log_beta = jnp.minimum(z, 0.0) - soft
    u = log_beta - z
    if mask is not None:
        u = jnp.where(mask, u, 0.0)
        log_beta = jnp.where(mask, log_beta, MASKED_LOG_BETA)
    u_hi = u.astype(BF16)
    u_lo = (u - u_hi.astype(F32)).astype(BF16)
    return log_beta, jnp.concatenate([u_hi, u_lo], axis=1), jnp.sum(u, axis=-1, keepdims=True)


def _sb_weights(log_beta, after, carry):
    return jnp.exp2(log_beta + carry + after).astype(BF16)


def _sb_key_run(q, ks, vs, tri2, carry, mask):
    n = len(ks)
    z = [_nt_dot(q, ks[i]) for i in range(n)]
    gates = [_sb_gates(z[i], mask) for i in range(n)]
    after = [_dot(gates[i][1], tri2) for i in range(n)]
    out = None
    for i in range(n):
        o = _dot(_sb_weights(gates[i][0], after[i], carry), vs[i])
        out = o if out is None else out + o
        carry = carry + gates[i][2]
    return out, carry


def _sb_doubled_gates(half_z, visible):
    t = jnp.tanh(half_z)
    beta2 = 1.0 + t
    rest2 = 1.0 - t
    if visible is not None:
        beta2 = jnp.where(visible, beta2, 0.0)
        rest2 = jnp.where(visible, rest2, 2.0)
    return beta2, rest2


def _sb_run_scan(half_z, visible):
    groups = half_z.shape[0] // SUBLANES
    pieces = [None] * groups
    run = None
    for r in reversed(range(groups)):
        rows = slice(r * SUBLANES, (r + 1) * SUBLANES)
        beta2, rest2 = _sb_doubled_gates(half_z[rows], None if visible is None else visible[rows])
        pieces[r] = beta2 if run is None else beta2 * run
        run = rest2 if run is None else run * rest2
    return pieces, run * 2.0 ** -groups


def _value_scale(n):
    groups = n // SUBLANES
    return jnp.exp2((jnp.arange(n) // SUBLANES - groups).astype(F32))


def _suffix_over_sublanes(t):
    s = lax.broadcasted_iota(jnp.int32, t.shape, 0)
    y = t
    for k in (1, 2, 4):
        below = pltpu.roll(y, SUBLANES - k, axis=0)
        y = y * jnp.where(s + k < SUBLANES, below, 1.0)
    exclusive = jnp.where(s + 1 < SUBLANES, pltpu.roll(y, SUBLANES - 1, axis=0), 1.0)
    return exclusive, jnp.broadcast_to(y[0:1, :], t.shape)


def _sb_finish(pieces, totals, v_t, carry):
    later_runs, total = _suffix_over_sublanes(totals)
    factor = later_runs * carry
    groups = pieces.shape[0] // SUBLANES
    w = jnp.concatenate([pieces[r * SUBLANES:(r + 1) * SUBLANES] * factor for r in range(groups)], axis=0)
    return _dot(v_t, w.astype(BF16)), carry * total


def _sb_prompt_kernel(q_ref, k_ref, vt_ref, o_ref, acc_ref, carry_ref, z_ref, pieces_ref, totals_ref,
                      *, blk):
    i = pl.program_id(1)
    heads = SB_HEADS
    cols = [slice(hd * SB_HEAD_DIM, (hd + 1) * SB_HEAD_DIM) for hd in range(heads)]
    parked = (heads - 2, heads - 1)

    def logits(block, hd):
        start = pl.multiple_of(block * blk, blk)
        return _nt_dot(k_ref[pl.ds(start, blk), cols[hd]], q_ref[:, cols[hd]])

    def finish(pieces, totals, block, hd):
        out, carry = _sb_finish(pieces, totals, vt_ref[block, cols[hd], :], carry_ref[hd])
        acc_ref[hd] += out
        carry_ref[hd] = carry

    def key_block(block, visible, have_parked):
        z_queue = [z_ref[ahead] for ahead in range(LOGITS_AHEAD)]
        scans = {}
        for hd in range(heads):
            ahead = [hd + LOGITS_AHEAD] if hd + LOGITS_AHEAD < heads else []
            if hd == heads - LOGITS_AHEAD - 1:
                ahead += range(heads, heads + LOGITS_AHEAD)
            for nxt in ahead:
                z_queue.append(logits(block, nxt) if nxt < heads
                               else logits(jnp.maximum(block - 1, 0), nxt - heads))
            if hd < len(parked):
                if have_parked:
                    finish(pieces_ref[hd], totals_ref[hd], block + 1, parked[hd])
            else:
                finish(*scans.pop(hd - len(parked)), block, hd - len(parked))
            pieces, totals = _sb_run_scan(z_queue.pop(0), visible)
            scans[hd] = (jnp.concatenate(pieces, axis=0), totals)
        for slot, hd in enumerate(parked):
            pieces_ref[slot], totals_ref[slot] = scans[hd]
        for ahead in range(LOGITS_AHEAD):
            z_ref[ahead] = z_queue[ahead]

    acc_ref[...] = jnp.zeros_like(acc_ref)
    carry_ref[...] = jnp.ones_like(carry_ref)
    for ahead in range(LOGITS_AHEAD):
        z_ref[ahead] = logits(i, ahead)
    row = lax.broadcasted_iota(jnp.int32, (blk, blk), 0)
    key = (row & (SUBLANES - 1)) * (blk // SUBLANES) + (row >> (SUBLANES.bit_length() - 1))
    key_block(i, key < lax.broadcasted_iota(jnp.int32, (blk, blk), 1), False)

    @pl.when(i % 2 == 1)
    def _():
        key_block(i - 1, None, True)

    def trip(t, _):
        top = i - 1 - i % 2 - 2 * t
        key_block(top, None, True)
        key_block(top - 1, None, True)
        return 0

    lax.fori_loop(0, i // 2, trip, 0)
    for slot, hd in enumerate(parked):
        finish(pieces_ref[slot], totals_ref[slot], 0, hd)
    for hd in range(heads):
        o_ref[:, cols[hd]] = acc_ref[hd].T.astype(o_ref.dtype)


def _sb_prompt(sq, sk_km, sv_t, batch, frames):
    blk = SB_BLOCK
    assert frames % blk == 0
    n_q = frames // blk
    q_tile = pl.BlockSpec((blk, D_MODEL), lambda b, i: (b * n_q + i, 0))
    k_all = pl.BlockSpec((frames, D_MODEL), lambda b, i: (b, 0))
    vt_all = pl.BlockSpec((None, n_q, D_MODEL, blk), lambda b, i: (b, 0, 0, 0))
    return pl.pallas_call(
        functools.partial(_sb_prompt_kernel, blk=blk),
        grid=(batch, n_q),
        in_specs=[q_tile, k_all, vt_all],
        out_specs=q_tile,
        out_shape=jax.ShapeDtypeStruct(sq.shape, BF16),
        scratch_shapes=[pltpu.VMEM((SB_HEADS, SB_HEAD_DIM, blk), F32),
                        pltpu.VMEM((SB_HEADS, SUBLANES, blk), F32),
                        pltpu.VMEM((LOGITS_AHEAD, blk, blk), F32),
                        pltpu.VMEM((2, blk, blk), F32),
                        pltpu.VMEM((2, SUBLANES, blk), F32)],
        compiler_params=pltpu.CompilerParams(
            dimension_semantics=("arbitrary", "arbitrary"), vmem_limit_bytes=VMEM_LIMIT_BYTES),
        name="sb_prompt",
    )(sq, sk_km, sv_t.reshape(batch, n_q, D_MODEL, blk))


def _sb_decode_kernel(q_ref, kn_ref, vn_ref, kc_ref, vc_ref, tri_ref, o_ref,
                      qbd_ref, acc_ref, carry_ref, *, new, key_tile, bk):
    kt = pl.program_id(1)
    rows = SB_HEADS * new
    tri2 = tri_ref[...]

    @pl.when(kt == 0)
    def _():
        q = q_ref[...]
        col = lax.broadcasted_iota(jnp.int32, (new, D_MODEL), 1)
        for hd in range(SB_HEADS):
            keep = (col >= hd * SB_HEAD_DIM) & (col < (hd + 1) * SB_HEAD_DIM)
            qbd_ref[hd * new:(hd + 1) * new, :] = jnp.where(keep, q, jnp.zeros_like(q))
        pad = kn_ref.shape[0]
        l_q = jnp.concatenate([lax.broadcasted_iota(jnp.int32, (new, pad), 0)] * SB_HEADS, axis=0)
        l_k = lax.broadcasted_iota(jnp.int32, (rows, pad), 1)
        tri2_pad = jnp.concatenate([tri2[:pad, :pad], tri2[bk:bk + pad, :pad]], axis=0)
        out, carry = _sb_key_run(qbd_ref[...], [kn_ref[...]], [vn_ref[...]], tri2_pad,
                                 jnp.zeros((rows, 1), F32), l_k < l_q)
        acc_ref[...] = out
        carry_ref[...] = carry

    def head_major(ref, sb):
        base = sb * bk * SB_HEADS
        return jnp.concatenate(
            [ref[pl.ds(base + hd, bk, stride=SB_HEADS), :].astype(BF16) for hd in range(SB_HEADS)], axis=1)

    order = list(reversed(range(key_tile // bk)))
    out, carry = _sb_key_run(qbd_ref[...], [head_major(kc_ref, sb) for sb in order],
                             [head_major(vc_ref, sb) for sb in order], tri2, carry_ref[...], None)
    acc_ref[...] += out
    carry_ref[...] = carry

    @pl.when(kt == pl.num_programs(1) - 1)
    def _():
        for hd in range(SB_HEADS):
            cols = slice(hd * SB_HEAD_DIM, (hd + 1) * SB_HEAD_DIM)
            o_ref[:, cols] = acc_ref[hd * new:(hd + 1) * new, cols].astype(o_ref.dtype)


def _sb_decode(sq, sk, sv, cache_k, cache_v, batch, new):
    past = cache_k.shape[1] // SB_HEADS
    key_tile = min(DECODE_KEY_TILE, past)
    bk = min(SB_BLOCK, key_tile)
    assert past % key_tile == 0 and key_tile % bk == 0 and new <= DECODE_NEW_PAD <= bk
    n_kt = past // key_tile
    pad = DECODE_NEW_PAD
    pad_keys = lambda a: jnp.pad(a.reshape(batch, new, D_MODEL), ((0, 0), (0, pad - new), (0, 0)))
    rows = SB_HEADS * new
    q_tile = pl.BlockSpec((new, D_MODEL), lambda b, kt: (b, 0))
    new_tile = pl.BlockSpec((None, pad, D_MODEL), lambda b, kt: (b, 0, 0))
    cache_tile = pl.BlockSpec((None, key_tile * SB_HEADS, SB_HEAD_DIM),
                              lambda b, kt: (b, n_kt - 1 - kt, 0))
    return pl.pallas_call(
        functools.partial(_sb_decode_kernel, new=new, key_tile=key_tile, bk=bk),
        grid=(batch, n_kt),
        in_specs=[q_tile, new_tile, new_tile, cache_tile, cache_tile, _resident((2 * bk, bk))],
        out_specs=q_tile,
        out_shape=jax.ShapeDtypeStruct(sq.shape, BF16),
        scratch_shapes=[pltpu.VMEM((rows, D_MODEL), BF16), pltpu.VMEM((rows, D_MODEL), F32),
                        pltpu.VMEM((rows, 1), F32)],
        compiler_params=pltpu.CompilerParams(
            dimension_semantics=("arbitrary", "arbitrary"), vmem_limit_bytes=VMEM_LIMIT_BYTES),
        name="sb_decode",
    )(sq, pad_keys(sk), pad_keys(sv), cache_k, cache_v, _suffix_matrix(bk))


def _out_mlp_kernel(x_ref, r_ref, s_ref, g2_ref, wo_ref, nw_ref, wu_ref, wd_ref, y_ref):
    mix = r_ref[...].astype(F32) + g2_ref[...].astype(F32) * s_ref[...].astype(F32)
    x1 = x_ref[...] + _dot(mix.astype(BF16), wo_ref[...])
    h = (_rms(x1) * nw_ref[...]).astype(BF16)
    up = jnp.maximum(_dot(h, wu_ref[...]), 0.0)
    y_ref[...] = x1 + _dot((up * up).astype(BF16), wd_ref[...])


def _out_mlp(x, r, s, g2, w_out_bf, norm_w, w_up_bf, w_down_bf):
    n = x.shape[0]
    tm = TOKEN_TILE
    assert n % tm == 0
    tile = pl.BlockSpec((tm, D_MODEL), lambda i: (i, 0))
    return pl.pallas_call(
        _out_mlp_kernel,
        grid=(n // tm,),
        in_specs=[tile, tile, tile, tile, _resident((D_MODEL, D_MODEL)), _resident((1, D_MODEL)),
                  _resident((D_MODEL, D_FF)), _resident((D_FF, D_MODEL))],
        out_specs=tile,
        out_shape=jax.ShapeDtypeStruct((n, D_MODEL), F32),
        compiler_params=pltpu.CompilerParams(
            dimension_semantics=("arbitrary",), vmem_limit_bytes=VMEM_LIMIT_BYTES),
        name="out_mlp",
    )(x, r, s, g2, w_out_bf, norm_w, w_up_bf, w_down_bf)


def _rope_tables(pos):
    half = RET_QK_DIM // 2
    inv_freq = ROPE_BASE ** (-jnp.arange(half, dtype=F32) / half)
    ang = pos.astype(F32)[:, None] * inv_freq[None, :]
    return jnp.cos(ang), jnp.sin(ang)


def kernel(x_prompt, x_sample, cache_sb_k, cache_sb_v, state_ret, norm_mix_w, w_in, ret_norm_w,
           sb_q_norm_w, sb_k_norm_w, w_out, norm_mlp_w, w_up, w_down):
    depth = w_in.shape[0]
    batch, frames, _ = x_prompt.shape
    dec_batch, new, _ = x_sample.shape
    past = cache_sb_k.shape[2]

    cos_p, sin_p = _rope_tables(jnp.arange(frames, dtype=jnp.int32))
    cos_s, sin_s = _rope_tables(past + jnp.arange(new, dtype=jnp.int32))
    reps = max(TOKEN_TILE // new, 1)
    cos_s, sin_s = jnp.tile(cos_s, (reps, 1)), jnp.tile(sin_s, (reps, 1))

    xp = x_prompt.reshape(batch * frames, D_MODEL)
    xs = x_sample.reshape(dec_batch * new, D_MODEL)
    outs = [[] for _ in range(6)]
    for l in range(depth):
        w_in_bf = w_in[l].astype(BF16)
        w_out_bf = w_out[l].astype(BF16)
        w_up_bf = w_up[l].astype(BF16)
        w_down_bf = w_down[l].astype(BF16)
        nmix = norm_mix_w[l].reshape(1, D_MODEL)
        nmlp = norm_mlp_w[l].reshape(1, D_MODEL)
        qnw = sb_q_norm_w[l].reshape(1, SB_HEAD_DIM)
        knw = sb_k_norm_w[l].reshape(1, SB_HEAD_DIM)

        rq, rk, rv, g1, sq, sk, skb, sv, svb, g2 = _project(xp, nmix, w_in_bf, cos_p, sin_p, qnw, knw, True)
        r, state_p = _retention(rq, rk, rv, g1, ret_norm_w[l], None, batch, frames)
        o_sb = _sb_prompt(sq, skb, svb, batch, frames)
        xp = _out_mlp(xp, r, o_sb, g2, w_out_bf, nmlp, w_up_bf, w_down_bf)
        outs[0].append(sk.reshape(batch, frames, SB_HEADS, SB_HEAD_DIM))
        outs[1].append(sv.reshape(batch, frames, SB_HEADS, SB_HEAD_DIM))
        outs[2].append(state_p.astype(state_ret.dtype))

        rq, rk, rv, g1, sq, sk, skb, sv, svb, g2 = _project(xs, nmix, w_in_bf, cos_s, sin_s, qnw, knw, False)
        r, state_s = _retention(rq, rk, rv, g1, ret_norm_w[l], state_ret[l].astype(F32), dec_batch, new)
        o_sb = _sb_decode(sq, skb, svb,
                          cache_sb_k[l].reshape(dec_batch, past * SB_HEADS, SB_HEAD_DIM),
                          cache_sb_v[l].reshape(dec_batch, past * SB_HEADS, SB_HEAD_DIM), dec_batch, new)
        xs = _out_mlp(xs, r, o_sb, g2, w_out_bf, nmlp, w_up_bf, w_down_bf)
        outs[3].append(sk.reshape(dec_batch, new, SB_HEADS, SB_HEAD_DIM))
        outs[4].append(sv.reshape(dec_batch, new, SB_HEADS, SB_HEAD_DIM))
        outs[5].append(state_s.astype(state_ret.dtype))

    return (xp.reshape(batch, frames, D_MODEL), xs.reshape(dec_batch, new, D_MODEL),
            jnp.stack(outs[0]), jnp.stack(outs[1]), jnp.stack(outs[2]),
            jnp.stack(outs[3]), jnp.stack(outs[4]), jnp.stack(outs[5]))
```

```python
import functools
import math

import jax
import jax.numpy as jnp
from jax import lax
from jax.experimental import pallas as pl
from jax.experimental.pallas import tpu as pltpu

D_MODEL = 1024
CHUNK = 64
RET_HEADS = 4
RET_QK_DIM = 256
RET_V_DIM = D_MODEL // RET_HEADS
SB_HEADS = 8
SB_HEAD_DIM = D_MODEL // SB_HEADS
D_FF = 4 * D_MODEL
ROPE_BASE = 10000.0
EPS = 1e-6
LOG2_E = math.log2(math.e)
N_GROUPS = 9

F32 = jnp.float32
BF16 = jnp.bfloat16

VMEM_LIMIT_BYTES = 56 * 1024 * 1024
TOKEN_TILE = 256
RET_BLOCK = 256
RET_STEP = 1024
RET_HEADS_PER_STEP = 2
SB_BLOCK = 256
SCAN_SEGMENT = 8
LOGITS_AHEAD = 2
SUBLANES = 8
SB_PROMPT_QUERY_SCALE = SB_HEAD_DIM ** -0.5 * 0.5
SB_DECODE_QUERY_SCALE = SB_HEAD_DIM ** -0.5 * LOG2_E
MASKED_LOG_BETA = -1e30
DECODE_KEY_TILE = 1024
DECODE_NEW_PAD = 128


def _nt_dot(a, b):
    return lax.dot_general(a, b, (((1,), (1,)), ((), ())), preferred_element_type=F32)


def _tn_dot(a, b):
    return lax.dot_general(a, b, (((0,), (0,)), ((), ())), preferred_element_type=F32)


def _dot(a, b):
    return jnp.dot(a, b, preferred_element_type=F32)


def _sigmoid(x):
    return 1.0 / (1.0 + jnp.exp(-x))


def _rms(x):
    return x * lax.rsqrt(jnp.mean(x * x, axis=-1, keepdims=True) + EPS)


def _resident(shape):
    return pl.BlockSpec(shape, lambda *_: (0,) * len(shape), pipeline_mode=pl.Buffered(1))


def _proj_kernel(*refs, key_major):
    if key_major:
        (x_ref, nw_ref, w_ref, cos_ref, sin_ref, qnw_ref, knw_ref, perm_ref, perm_t_ref,
         rq_ref, rk_ref, rv_ref, g1_ref, sq_ref, skf_ref, skb_ref, svf_ref, svb_ref, g2_ref) = refs
    else:
        (x_ref, nw_ref, w_ref, cos_ref, sin_ref, qnw_ref, knw_ref,
         rq_ref, rk_ref, rv_ref, g1_ref, sq_ref, skf_ref, skb_ref, svf_ref, svb_ref, g2_ref) = refs
    h = (_rms(x_ref[...]) * nw_ref[...]).astype(BF16)

    def group(g):
        return _dot(h, w_ref[:, g * D_MODEL:(g + 1) * D_MODEL])

    cos = cos_ref[...]
    sin = sin_ref[...]
    half = RET_QK_DIM // 2

    def rotary(p, out_ref, scale):
        for hd in range(RET_HEADS):
            lo = hd * RET_QK_DIM
            x1 = p[:, lo:lo + half]
            x2 = p[:, lo + half:lo + 2 * half]
            o1 = x1 * cos - x2 * sin
            o2 = x1 * sin + x2 * cos
            if scale != 1.0:
                o1 = o1 * scale
                o2 = o2 * scale
            out_ref[:, lo:lo + half] = o1.astype(out_ref.dtype)
            out_ref[:, lo + half:lo + 2 * half] = o2.astype(out_ref.dtype)

    def head_norm(p, w, scale=1.0):
        ys = []
        for hd in range(SB_HEADS):
            y = _rms(p[:, hd * SB_HEAD_DIM:(hd + 1) * SB_HEAD_DIM]) * w
            ys.append(y if scale == 1.0 else y * scale)
        return jnp.concatenate(ys, axis=1)

    rotary(group(0), rq_ref, 1.0)
    rotary(group(1), rk_ref, RET_QK_DIM ** -0.5)
    rv_ref[...] = group(2).astype(BF16)
    rg = group(3)
    query_scale = SB_PROMPT_QUERY_SCALE if key_major else SB_DECODE_QUERY_SCALE
    sq_ref[...] = head_norm(group(4), qnw_ref[...], query_scale).astype(BF16)
    sk = head_norm(group(5), knw_ref[...])
    skf_ref[...] = sk
    sv = group(6)
    svf_ref[...] = sv
    if key_major:
        skb_ref[...] = _dot(perm_ref[...], sk.astype(BF16)).astype(BF16)
        svb_ref[...] = _tn_dot(sv.astype(BF16), perm_t_ref[...]).astype(BF16)
    else:
        skb_ref[...] = sk.astype(BF16)
        svb_ref[...] = sv.astype(BF16)
    ga = group(7)
    g1_ref[...] = (rg * _sigmoid(rg) * _sigmoid(ga)).astype(BF16)
    g2_ref[...] = _sigmoid(group(8)).astype(BF16)


def _key_order(n):
    row = jnp.arange(n)
    key = (row % SUBLANES) * (n // SUBLANES) + row // SUBLANES
    return (key[:, None] == jnp.arange(n)[None, :]).astype(BF16)


def _project(x, norm_w, w_in_bf, cos_tab, sin_tab, qnw, knw, key_major):
    n = x.shape[0]
    tm = TOKEN_TILE
    assert n % tm == 0 and cos_tab.shape[0] % tm == 0
    pos_blocks = cos_tab.shape[0] // tm
    tile = pl.BlockSpec((tm, D_MODEL), lambda i: (i, 0))
    pos_tile = pl.BlockSpec((tm, RET_QK_DIM // 2), lambda i: (i % pos_blocks, 0))
    bf = jax.ShapeDtypeStruct((n, D_MODEL), BF16)
    f32 = jax.ShapeDtypeStruct((n, D_MODEL), F32)
    in_specs = [tile, _resident((1, D_MODEL)), _resident((D_MODEL, N_GROUPS * D_MODEL)),
                pos_tile, pos_tile, _resident((1, SB_HEAD_DIM)), _resident((1, SB_HEAD_DIM))]
    args = [x, norm_w, w_in_bf, cos_tab, sin_tab, qnw, knw]
    out_specs = [tile] * 10
    out_shape = [bf, bf, bf, bf, bf, f32, bf, f32, bf, bf]
    if key_major:
        assert tm == SB_BLOCK
        perm = _key_order(tm)
        in_specs += [_resident((tm, tm)), _resident((tm, tm))]
        args += [perm, (perm.T.astype(F32) * _value_scale(tm)[None, :]).astype(BF16)]
        out_specs[8] = pl.BlockSpec((None, D_MODEL, tm), lambda i: (i, 0, 0))
        out_shape[8] = jax.ShapeDtypeStruct((n // tm, D_MODEL, tm), BF16)
    return pl.pallas_call(
        functools.partial(_proj_kernel, key_major=key_major),
        grid=(n // tm,),
        in_specs=in_specs,
        out_specs=out_specs,
        out_shape=out_shape,
        compiler_params=pltpu.CompilerParams(
            dimension_semantics=("arbitrary",), vmem_limit_bytes=VMEM_LIMIT_BYTES),
        name="proj",
    )(*args)


def _retention_kernel(*refs, n_sub, block, heads, has_state):
    if has_state:
        gl_ref, q_ref, k_ref, v_ref, g1_ref, dec_ref, qd_ref, kd_ref, rnw_ref, s0_ref = refs[:10]
        rest = refs[10:]
    else:
        gl_ref, q_ref, k_ref, v_ref, g1_ref, dec_ref, qd_ref, kd_ref, rnw_ref = refs[:9]
        s0_ref = None
        rest = refs[9:]
    r_ref, sout_ref, s_ref = rest
    first_head = pl.program_id(1) * heads
    t = pl.program_id(2)

    @pl.when(t == 0)
    def _():
        if has_state:
            s_ref[...] = s0_ref[...]
        else:
            s_ref[...] = jnp.zeros_like(s_ref)

    hs = range(heads)
    cols = [slice(j * RET_QK_DIM, (j + 1) * RET_QK_DIM) for j in hs]
    for sb in range(n_sub):
        rows = slice(sb * block, (sb + 1) * block)
        q = [q_ref[rows, c] for c in cols]
        k = [k_ref[rows, c] for c in cols]
        v = [v_ref[rows, c] for c in cols]
        s_prev = [s_ref[j] for j in hs]
        scores = [(_nt_dot(q[j], k[j]) * dec_ref[j]).astype(BF16) for j in hs]
        cross = [_dot(q[j], s_prev[j].astype(BF16)) * qd_ref[j] for j in hs]
        k_dec = [(k[j].astype(F32) * kd_ref[j]).astype(BF16) for j in hs]
        for j in hs:
            s_ref[j] = gl_ref[first_head + j] * s_prev[j] + _tn_dot(k_dec[j], v[j])
        for j in hs:
            r = _rms(_dot(scores[j], v[j]) + cross[j]) * rnw_ref[j]
            r_ref[rows, cols[j]] = (r * g1_ref[rows, cols[j]].astype(F32)).astype(BF16)

    @pl.when(t == pl.num_programs(2) - 1)
    def _():
        sout_ref[...] = s_ref[...]


def _retention_tables(block):
    log_gamma = jnp.log1p(-jnp.exp2(-5.0 - jnp.arange(RET_HEADS, dtype=F32)))
    idx = jnp.arange(block, dtype=F32)
    dist = jnp.abs(idx[:, None] - idx[None, :])
    chunk = jnp.arange(block, dtype=jnp.int32) // CHUNK
    visible = chunk[None, :] <= chunk[:, None]
    dec = jnp.where(visible[None], jnp.exp(log_gamma[:, None, None] * dist[None]), 0.0)
    qd = jnp.exp(log_gamma[:, None] * (idx[None, :] + 1.0))
    kd = jnp.exp(log_gamma[:, None] * (block - 1.0 - idx[None, :]))
    widen = lambda a: jnp.broadcast_to(a[:, :, None], (RET_HEADS, block, RET_QK_DIM))
    gl = jnp.exp(log_gamma * block)
    return gl, dec, widen(qd), widen(kd)


def _retention(rq, rk, rv, g1, ret_norm_w, state0, batch, frames):
    block = min(RET_BLOCK, frames)
    step = min(RET_STEP, frames)
    assert frames % step == 0 and step % block == 0
    assert block % CHUNK == 0 or block < CHUNK
    n_t = frames // step
    gl, dec, qd, kd = _retention_tables(block)
    has_state = state0 is not None
    heads = RET_HEADS_PER_STEP
    assert RET_HEADS % heads == 0
    tile = pl.BlockSpec((step, heads * RET_QK_DIM), lambda b, h, t: (b * n_t + t, h))
    per_head = lambda rows, cols: pl.BlockSpec((heads, rows, cols), lambda b, h, t: (h, 0, 0))
    state_spec = pl.BlockSpec((None, heads, RET_QK_DIM, RET_V_DIM), lambda b, h, t: (b, h, 0, 0))
    in_specs = [pl.BlockSpec(memory_space=pltpu.SMEM), tile, tile, tile, tile,
                per_head(block, block), per_head(block, RET_QK_DIM), per_head(block, RET_QK_DIM),
                per_head(1, RET_V_DIM)]
    args = [gl, rq, rk, rv, g1, dec, qd, kd, ret_norm_w.reshape(RET_HEADS, 1, RET_V_DIM)]
    if has_state:
        in_specs.append(state_spec)
        args.append(state0)
    return pl.pallas_call(
        functools.partial(_retention_kernel, n_sub=step // block, block=block, heads=heads,
                          has_state=has_state),
        grid=(batch, RET_HEADS // heads, n_t),
        in_specs=in_specs,
        out_specs=[tile, state_spec],
        out_shape=[jax.ShapeDtypeStruct(rq.shape, BF16),
                   jax.ShapeDtypeStruct((batch, RET_HEADS, RET_QK_DIM, RET_V_DIM), F32)],
        scratch_shapes=[pltpu.VMEM((heads, RET_QK_DIM, RET_V_DIM), F32)],
        compiler_params=pltpu.CompilerParams(
            dimension_semantics=("arbitrary", "arbitrary", "arbitrary"),
            vmem_limit_bytes=VMEM_LIMIT_BYTES),
        name="retention",
    )(*args)


def _suffix_matrix(n):
    idx = jnp.arange(n)
    m = (idx[:, None] > idx[None, :]).astype(BF16)
    return jnp.concatenate([m, m], axis=0)


def _sb_gates(z, mask):
    neg_abs = pltpu.bitcast(pltpu.bitcast(z, jnp.uint32) | jnp.uint32(0x80000000), F32)
    soft = jnp.log2(1.0 + jnp.exp2(neg_abs))
    log_beta = jnp.minimum(z, 0.0) - soft
    u = log_beta - z
    if mask is not None:
        u = jnp.where(mask, u, 0.0)
        log_beta = jnp.where(mask, log_beta, MASKED_LOG_BETA)
    u_hi = u.astype(BF16)
    u_lo = (u - u_hi.astype(F32)).astype(BF16)
    return log_beta, jnp.concatenate([u_hi, u_lo], axis=1), jnp.sum(u, axis=-1, keepdims=True)


def _sb_weights(log_beta, after, carry):
    return jnp.exp2(log_beta + carry + after).astype(BF16)


def _sb_key_run(q, ks, vs, tri2, carry, mask):
    n = len(ks)
    z = [_nt_dot(q, ks[i]) for i in range(n)]
    gates = [_sb_gates(z[i], mask) for i in range(n)]
    after = [_dot(gates[i][1], tri2) for i in range(n)]
    out = None
    for i in range(n):
        o = _dot(_sb_weights(gates[i][0], after[i], carry), vs[i])
        out = o if out is None else out + o
        carry = carry + gates[i][2]
    return out, carry


def _sb_doubled_gates(half_z, visible):
    t = jnp.tanh(half_z)
    beta2 = 1.0 + t
    rest2 = 1.0 - t
    if visible is not None:
        beta2 = jnp.where(visible, beta2, 0.0)
        rest2 = jnp.where(visible, rest2, 2.0)
    return beta2, rest2


def _sb_run_scan(half_z, visible):
    groups = half_z.shape[0] // SUBLANES
    pieces = [None] * groups
    totals = []
    for first in range(0, groups, SCAN_SEGMENT):
        run = None
        for r in reversed(range(first, first + SCAN_SEGMENT)):
            rows = slice(r * SUBLANES, (r + 1) * SUBLANES)
            beta2, rest2 = _sb_doubled_gates(half_z[rows], None if visible is None else visible[rows])
            pieces[r] = beta2 if run is None else beta2 * run
            run = rest2 if run is None else run * rest2
        totals.append(run * 2.0 ** -SCAN_SEGMENT)
    return pieces, jnp.concatenate(totals, axis=0)


def _value_scale(n):
    place = (jnp.arange(n) // SUBLANES) % SCAN_SEGMENT
    return jnp.exp2((place - SCAN_SEGMENT).astype(F32))


def _suffix_over_sublanes(t):
    s = lax.broadcasted_iota(jnp.int32, t.shape, 0)
    y = t
    for k in (1, 2, 4):
        below = pltpu.roll(y, SUBLANES - k, axis=0)
        y = y * jnp.where(s + k < SUBLANES, below, 1.0)
    exclusive = jnp.where(s + 1 < SUBLANES, pltpu.roll(y, SUBLANES - 1, axis=0), 1.0)
    return exclusive, jnp.broadcast_to(y[0:1, :], t.shape)


def _sb_finish(pieces, totals, v_t, carry):
    segments = totals.shape[0] // SUBLANES
    seg_totals = [totals[j * SUBLANES:(j + 1) * SUBLANES] for j in range(segments)]
    later_segments = [None] * segments
    run_total = None
    for j in reversed(range(segments)):
        later_segments[j] = run_total
        run_total = seg_totals[j] if run_total is None else run_total * seg_totals[j]
    later_runs, total = _suffix_over_sublanes(run_total)
    base = later_runs * carry
    slab = 2 * SUBLANES
    slabs_per_segment = pieces.shape[0] // slab // segments
    w = []
    for j in range(segments):
        factor = base if later_segments[j] is None else base * later_segments[j]
        factor = jnp.concatenate([factor, factor], axis=0).astype(BF16)
        for g in range(j * slabs_per_segment, (j + 1) * slabs_per_segment):
            w.append(pieces[g * slab:(g + 1) * slab] * factor)
    return _dot(v_t, jnp.concatenate(w, axis=0)), carry * total


def _sb_prompt_kernel(q_ref, k_ref, vt_ref, o_ref, acc_ref, carry_ref, z_ref, pieces_ref, totals_ref,
                      *, blk):
    i = pl.program_id(1)
    heads = SB_HEADS
    cols = [slice(hd * SB_HEAD_DIM, (hd + 1) * SB_HEAD_DIM) for hd in range(heads)]
    parked = (heads - 2, heads - 1)

    def logits(block, hd):
        start = pl.multiple_of(block * blk, blk)
        return _nt_dot(k_ref[pl.ds(start, blk), cols[hd]], q_ref[:, cols[hd]])

    def finish(pieces, totals, block, hd):
        out, carry = _sb_finish(pieces, totals, vt_ref[block, cols[hd], :], carry_ref[hd])
        acc_ref[hd] += out
        carry_ref[hd] = carry

    def key_block(block, visible, have_parked):
        z_queue = [z_ref[ahead] for ahead in range(LOGITS_AHEAD)]
        scans = {}
        for hd in range(heads):
            ahead = [hd + LOGITS_AHEAD] if hd + LOGITS_AHEAD < heads else []
            if hd == heads - LOGITS_AHEAD - 1:
                ahead += range(heads, heads + LOGITS_AHEAD)
            for nxt in ahead:
                z_queue.append(logits(block, nxt) if nxt < heads
                               else logits(jnp.maximum(block - 1, 0), nxt - heads))
            if hd < len(parked):
                if have_parked:
                    finish(pieces_ref[hd], totals_ref[hd], block + 1, parked[hd])
            else:
                finish(*scans.pop(hd - len(parked)), block, hd - len(parked))
            pieces, totals = _sb_run_scan(z_queue.pop(0), visible)
            scans[hd] = (jnp.concatenate(pieces, axis=0).astype(BF16), totals)
        for slot, hd in enumerate(parked):
            pieces_ref[slot], totals_ref[slot] = scans[hd]
        for ahead in range(LOGITS_AHEAD):
            z_ref[ahead] = z_queue[ahead]

    acc_ref[...] = jnp.zeros_like(acc_ref)
    carry_ref[...] = jnp.ones_like(carry_ref)
    for ahead in range(LOGITS_AHEAD):
        z_ref[ahead] = logits(i, ahead)
    row = lax.broadcasted_iota(jnp.int32, (blk, blk), 0)
    key = (row & (SUBLANES - 1)) * (blk // SUBLANES) + (row >> (SUBLANES.bit_length() - 1))
    key_block(i, key < lax.broadcasted_iota(jnp.int32, (blk, blk), 1), False)

    @pl.when(i % 2 == 1)
    def _():
        key_block(i - 1, None, True)

    def trip(t, _):
        top = i - 1 - i % 2 - 2 * t
        key_block(top, None, True)
        key_block(top - 1, None, True)
        return 0

    lax.fori_loop(0, i // 2, trip, 0)
    for slot, hd in enumerate(parked):
        finish(pieces_ref[slot], totals_ref[slot], 0, hd)
    for hd in range(heads):
        o_ref[:, cols[hd]] = acc_ref[hd].T.astype(o_ref.dtype)


def _sb_prompt(sq, sk_km, sv_t, batch, frames):
    blk = SB_BLOCK
    assert frames % blk == 0
    n_q = frames // blk
    q_tile = pl.BlockSpec((blk, D_MODEL), lambda b, i: (b * n_q + i, 0))
    k_all = pl.BlockSpec((frames, D_MODEL), lambda b, i: (b, 0))
    vt_all = pl.BlockSpec((None, n_q, D_MODEL, blk), lambda b, i: (b, 0, 0, 0))
    return pl.pallas_call(
        functools.partial(_sb_prompt_kernel, blk=blk),
        grid=(batch, n_q),
        in_specs=[q_tile, k_all, vt_all],
        out_specs=q_tile,
        out_shape=jax.ShapeDtypeStruct(sq.shape, BF16),
        scratch_shapes=[pltpu.VMEM((SB_HEADS, SB_HEAD_DIM, blk), F32),
                        pltpu.VMEM((SB_HEADS, SUBLANES, blk), F32),
                        pltpu.VMEM((LOGITS_AHEAD, blk, blk), F32),
                        pltpu.VMEM((2, blk, blk), BF16),
                        pltpu.VMEM((2, blk // SCAN_SEGMENT, blk), F32)],
        compiler_params=pltpu.CompilerParams(
            dimension_semantics=("arbitrary", "arbitrary"), vmem_limit_bytes=VMEM_LIMIT_BYTES),
        name="sb_prompt",
    )(sq, sk_km, sv_t.reshape(batch, n_q, D_MODEL, blk))


def _sb_decode_kernel(q_ref, kn_ref, vn_ref, kc_ref, vc_ref, tri_ref, o_ref,
                      qbd_ref, acc_ref, carry_ref, *, new, key_tile, bk):
    kt = pl.program_id(1)
    rows = SB_HEADS * new
    tri2 = tri_ref[...]

    @pl.when(kt == 0)
    def _():
        q = q_ref[...]
        col = lax.broadcasted_iota(jnp.int32, (new, D_MODEL), 1)
        for hd in range(SB_HEADS):
            keep = (col >= hd * SB_HEAD_DIM) & (col < (hd + 1) * SB_HEAD_DIM)
            qbd_ref[hd * new:(hd + 1) * new, :] = jnp.where(keep, q, jnp.zeros_like(q))
        pad = kn_ref.shape[0]
        l_q = jnp.concatenate([lax.broadcasted_iota(jnp.int32, (new, pad), 0)] * SB_HEADS, axis=0)
        l_k = lax.broadcasted_iota(jnp.int32, (rows, pad), 1)
        tri2_pad = jnp.concatenate([tri2[:pad, :pad], tri2[bk:bk + pad, :pad]], axis=0)
        out, carry = _sb_key_run(qbd_ref[...], [kn_ref[...]], [vn_ref[...]], tri2_pad,
                                 jnp.zeros((rows, 1), F32), l_k < l_q)
        acc_ref[...] = out
        carry_ref[...] = carry

    def head_major(ref, sb):
        base = sb * bk * SB_HEADS
        return jnp.concatenate(
            [ref[pl.ds(base + hd, bk, stride=SB_HEADS), :].astype(BF16) for hd in range(SB_HEADS)], axis=1)

    order = list(reversed(range(key_tile // bk)))
    out, carry = _sb_key_run(qbd_ref[...], [head_major(kc_ref, sb) for sb in order],
                             [head_major(vc_ref, sb) for sb in order], tri2, carry_ref[...], None)
    acc_ref[...] += out
    carry_ref[...] = carry

    @pl.when(kt == pl.num_programs(1) - 1)
    def _():
        for hd in range(SB_HEADS):
            cols = slice(hd * SB_HEAD_DIM, (hd + 1) * SB_HEAD_DIM)
            o_ref[:, cols] = acc_ref[hd * new:(hd + 1) * new, cols].astype(o_ref.dtype)


def _sb_decode(sq, sk, sv, cache_k, cache_v, batch, new):
    past = cache_k.shape[1] // SB_HEADS
    key_tile = min(DECODE_KEY_TILE, past)
    bk = min(SB_BLOCK, key_tile)
    assert past % key_tile == 0 and key_tile % bk == 0 and new <= DECODE_NEW_PAD <= bk
    n_kt = past // key_tile
    pad = DECODE_NEW_PAD
    pad_keys = lambda a: jnp.pad(a.reshape(batch, new, D_MODEL), ((0, 0), (0, pad - new), (0, 0)))
    rows = SB_HEADS * new
    q_tile = pl.BlockSpec((new, D_MODEL), lambda b, kt: (b, 0))
    new_tile = pl.BlockSpec((None, pad, D_MODEL), lambda b, kt: (b, 0, 0))
    cache_tile = pl.BlockSpec((None, key_tile * SB_HEADS, SB_HEAD_DIM),
                              lambda b, kt: (b, n_kt - 1 - kt, 0))
    return pl.pallas_call(
        functools.partial(_sb_decode_kernel, new=new, key_tile=key_tile, bk=bk),
        grid=(batch, n_kt),
        in_specs=[q_tile, new_tile, new_tile, cache_tile, cache_tile, _resident((2 * bk, bk))],
        out_specs=q_tile,
        out_shape=jax.ShapeDtypeStruct(sq.shape, BF16),
        scratch_shapes=[pltpu.VMEM((rows, D_MODEL), BF16), pltpu.VMEM((rows, D_MODEL), F32),
                        pltpu.VMEM((rows, 1), F32)],
        compiler_params=pltpu.CompilerParams(
            dimension_semantics=("arbitrary", "arbitrary"), vmem_limit_bytes=VMEM_LIMIT_BYTES),
        name="sb_decode",
    )(sq, pad_keys(sk), pad_keys(sv), cache_k, cache_v, _suffix_matrix(bk))


def _out_mlp_kernel(x_ref, r_ref, s_ref, g2_ref, wo_ref, nw_ref, wu_ref, wd_ref, y_ref):
    mix = r_ref[...].astype(F32) + g2_ref[...].astype(F32) * s_ref[...].astype(F32)
    x1 = x_ref[...] + _dot(mix.astype(BF16), wo_ref[...])
    h = (_rms(x1) * nw_ref[...]).astype(BF16)
    up = jnp.maximum(_dot(h, wu_ref[...]), 0.0)
    y_ref[...] = x1 + _dot((up * up).astype(BF16), wd_ref[...])


def _out_mlp(x, r, s, g2, w_out_bf, norm_w, w_up_bf, w_down_bf):
    n = x.shape[0]
    tm = TOKEN_TILE
    assert n % tm == 0
    tile = pl.BlockSpec((tm, D_MODEL), lambda i: (i, 0))
    return pl.pallas_call(
        _out_mlp_kernel,
        grid=(n // tm,),
        in_specs=[tile, tile, tile, tile, _resident((D_MODEL, D_MODEL)), _resident((1, D_MODEL)),
                  _resident((D_MODEL, D_FF)), _resident((D_FF, D_MODEL))],
        out_specs=tile,
        out_shape=jax.ShapeDtypeStruct((n, D_MODEL), F32),
        compiler_params=pltpu.CompilerParams(
            dimension_semantics=("arbitrary",), vmem_limit_bytes=VMEM_LIMIT_BYTES),
        name="out_mlp",
    )(x, r, s, g2, w_out_bf, norm_w, w_up_bf, w_down_bf)


def _rope_tables(pos):
    half = RET_QK_DIM // 2
    inv_freq = ROPE_BASE ** (-jnp.arange(half, dtype=F32) / half)
    ang = pos.astype(F32)[:, None] * inv_freq[None, :]
    return jnp.cos(ang), jnp.sin(ang)


def kernel(x_prompt, x_sample, cache_sb_k, cache_sb_v, state_ret, norm_mix_w, w_in, ret_norm_w,
           sb_q_norm_w, sb_k_norm_w, w_out, norm_mlp_w, w_up, w_down):
    depth = w_in.shape[0]
    batch, frames, _ = x_prompt.shape
    dec_batch, new, _ = x_sample.shape
    past = cache_sb_k.shape[2]

    cos_p, sin_p = _rope_tables(jnp.arange(frames, dtype=jnp.int32))
    cos_s, sin_s = _rope_tables(past + jnp.arange(new, dtype=jnp.int32))
    reps = max(TOKEN_TILE // new, 1)
    cos_s, sin_s = jnp.tile(cos_s, (reps, 1)), jnp.tile(sin_s, (reps, 1))

    xp = x_prompt.reshape(batch * frames, D_MODEL)
    xs = x_sample.reshape(dec_batch * new, D_MODEL)
    outs = [[] for _ in range(6)]
    for l in range(depth):
        w_in_bf = w_in[l].astype(BF16)
        w_out_bf = w_out[l].astype(BF16)
        w_up_bf = w_up[l].astype(BF16)
        w_down_bf = w_down[l].astype(BF16)
        nmix = norm_mix_w[l].reshape(1, D_MODEL)
        nmlp = norm_mlp_w[l].reshape(1, D_MODEL)
        qnw = sb_q_norm_w[l].reshape(1, SB_HEAD_DIM)
        knw = sb_k_norm_w[l].reshape(1, SB_HEAD_DIM)

        rq, rk, rv, g1, sq, sk, skb, sv, svb, g2 = _project(xp, nmix, w_in_bf, cos_p, sin_p, qnw, knw, True)
        r, state_p = _retention(rq, rk, rv, g1, ret_norm_w[l], None, batch, frames)
        o_sb = _sb_prompt(sq, skb, svb, batch, frames)
        xp = _out_mlp(xp, r, o_sb, g2, w_out_bf, nmlp, w_up_bf, w_down_bf)
        outs[0].append(sk.reshape(batch, frames, SB_HEADS, SB_HEAD_DIM))
        outs[1].append(sv.reshape(batch, frames, SB_HEADS, SB_HEAD_DIM))
        outs[2].append(state_p.astype(state_ret.dtype))

        rq, rk, rv, g1, sq, sk, skb, sv, svb, g2 = _project(xs, nmix, w_in_bf, cos_s, sin_s, qnw, knw, False)
        r, state_s = _retention(rq, rk, rv, g1, ret_norm_w[l], state_ret[l].astype(F32), dec_batch, new)
        o_sb = _sb_decode(sq, skb, svb,
                          cache_sb_k[l].reshape(dec_batch, past * SB_HEADS, SB_HEAD_DIM),
                          cache_sb_v[l].reshape(dec_batch, past * SB_HEADS, SB_HEAD_DIM), dec_batch, new)
        xs = _out_mlp(xs, r, o_sb, g2, w_out_bf, nmlp, w_up_bf, w_down_bf)
        outs[3].append(sk.reshape(dec_batch, new, SB_HEADS, SB_HEAD_DIM))
        outs[4].append(sv.reshape(dec_batch, new, SB_HEADS, SB_HEAD_DIM))
        outs[5].append(state_s.astype(state_ret.dtype))

    return (xp.reshape(batch, frames, D_MODEL), xs.reshape(dec_batch, new, D_MODEL),
            jnp.stack(outs[0]), jnp.stack(outs[1]), jnp.stack(outs[2]),
            jnp.stack(outs[3]), jnp.stack(outs[4]), jnp.stack(outs[5]))
```

```python
import functools
import math

import jax
import jax.numpy as jnp
from jax import lax
from jax.experimental import pallas as pl
from jax.experimental.pallas import tpu as pltpu

D_MODEL = 1024
CHUNK = 64
RET_HEADS = 4
RET_QK_DIM = 256
RET_V_DIM = D_MODEL // RET_HEADS
SB_HEADS = 8
SB_HEAD_DIM = D_MODEL // SB_HEADS
D_FF = 4 * D_MODEL
ROPE_BASE = 10000.0
EPS = 1e-6
LOG2_E = math.log2(math.e)
N_GROUPS = 9

F32 = jnp.float32
BF16 = jnp.bfloat16

VMEM_LIMIT_BYTES = 56 * 1024 * 1024
TOKEN_TILE = 256
OUT_MLP_CHAINS = 2
RET_BLOCK = 256
RET_STEP = 1024
RET_HEADS_PER_STEP = 2
RET_HEADS_PER_SHORT_STEP = 4
SB_BLOCK = 256
SCAN_SEGMENT = 8
LOGITS_AHEAD = 2
SUBLANES = 8
SB_PROMPT_QUERY_SCALE = SB_HEAD_DIM ** -0.5 * 0.5
SB_DECODE_QUERY_SCALE = SB_HEAD_DIM ** -0.5 * LOG2_E
MASKED_LOG_BETA = -1e30
DECODE_KEY_TILE = 1024
DECODE_NEW_PAD = 128


def _nt_dot(a, b):
    return lax.dot_general(a, b, (((1,), (1,)), ((), ())), preferred_element_type=F32)


def _tn_dot(a, b):
    return lax.dot_general(a, b, (((0,), (0,)), ((), ())), preferred_element_type=F32)


def _dot(a, b):
    return jnp.dot(a, b, preferred_element_type=F32)


def _sigmoid(x):
    return 1.0 / (1.0 + jnp.exp(-x))


def _rms(x):
    return x * lax.rsqrt(jnp.mean(x * x, axis=-1, keepdims=True) + EPS)


def _resident(shape):
    return pl.BlockSpec(shape, lambda *_: (0,) * len(shape), pipeline_mode=pl.Buffered(1))


def _proj_kernel(*refs, key_major):
    if key_major:
        (x_ref, nw_ref, w_ref, cos_ref, sin_ref, qnw_ref, knw_ref, perm_ref, perm_t_ref,
         rq_ref, rk_ref, rv_ref, g1_ref, sq_ref, skf_ref, skb_ref, svf_ref, svb_ref, g2_ref) = refs
    else:
        (x_ref, nw_ref, w_ref, cos_ref, sin_ref, qnw_ref, knw_ref,
         rq_ref, rk_ref, rv_ref, g1_ref, sq_ref, skf_ref, skb_ref, svf_ref, svb_ref, g2_ref) = refs
    h = (_rms(x_ref[...]) * nw_ref[...]).astype(BF16)

    def group(g):
        return _dot(h, w_ref[:, g * D_MODEL:(g + 1) * D_MODEL])

    cos = cos_ref[...]
    sin = sin_ref[...]
    half = RET_QK_DIM // 2

    def rotary(p, out_ref, scale):
        for hd in range(RET_HEADS):
            lo = hd * RET_QK_DIM
            x1 = p[:, lo:lo + half]
            x2 = p[:, lo + half:lo + 2 * half]
            o1 = x1 * cos - x2 * sin
            o2 = x1 * sin + x2 * cos
            if scale != 1.0:
                o1 = o1 * scale
                o2 = o2 * scale
            out_ref[:, lo:lo + half] = o1.astype(out_ref.dtype)
            out_ref[:, lo + half:lo + 2 * half] = o2.astype(out_ref.dtype)

    def head_norm(p, w, scale=1.0):
        ys = []
        for hd in range(SB_HEADS):
            y = _rms(p[:, hd * SB_HEAD_DIM:(hd + 1) * SB_HEAD_DIM]) * w
            ys.append(y if scale == 1.0 else y * scale)
        return jnp.concatenate(ys, axis=1)

    rotary(group(0), rq_ref, 1.0)
    rotary(group(1), rk_ref, RET_QK_DIM ** -0.5)
    rv_ref[...] = group(2).astype(BF16)
    rg = group(3)
    query_scale = SB_PROMPT_QUERY_SCALE if key_major else SB_DECODE_QUERY_SCALE
    sq_ref[...] = head_norm(group(4), qnw_ref[...], query_scale).astype(BF16)
    sk = head_norm(group(5), knw_ref[...])
    skf_ref[...] = sk
    sv = group(6)
    svf_ref[...] = sv
    if key_major:
        skb_ref[...] = _dot(perm_ref[...], sk.astype(BF16)).astype(BF16)
        svb_ref[...] = _tn_dot(sv.astype(BF16), perm_t_ref[...]).astype(BF16)
    else:
        skb_ref[...] = sk.astype(BF16)
        svb_ref[...] = sv.astype(BF16)
    ga = group(7)
    g1_ref[...] = (rg * _sigmoid(rg) * _sigmoid(ga)).astype(BF16)
    g2_ref[...] = _sigmoid(group(8)).astype(BF16)


def _key_order(n):
    row = jnp.arange(n)
    key = (row % SUBLANES) * (n // SUBLANES) + row // SUBLANES
    return (key[:, None] == jnp.arange(n)[None, :]).astype(BF16)


def _project(x, norm_w, w_in_bf, cos_tab, sin_tab, qnw, knw, key_major):
    n = x.shape[0]
    tm = TOKEN_TILE
    assert n % tm == 0 and cos_tab.shape[0] % tm == 0
    pos_blocks = cos_tab.shape[0] // tm
    tile = pl.BlockSpec((tm, D_MODEL), lambda i: (i, 0))
    pos_tile = pl.BlockSpec((tm, RET_QK_DIM // 2), lambda i: (i % pos_blocks, 0))
    bf = jax.ShapeDtypeStruct((n, D_MODEL), BF16)
    f32 = jax.ShapeDtypeStruct((n, D_MODEL), F32)
    in_specs = [tile, _resident((1, D_MODEL)), _resident((D_MODEL, N_GROUPS * D_MODEL)),
                pos_tile, pos_tile, _resident((1, SB_HEAD_DIM)), _resident((1, SB_HEAD_DIM))]
    args = [x, norm_w, w_in_bf, cos_tab, sin_tab, qnw, knw]
    out_specs = [tile] * 10
    out_shape = [bf, bf, bf, bf, bf, f32, bf, f32, bf, bf]
    if key_major:
        assert tm == SB_BLOCK
        perm = _key_order(tm)
        in_specs += [_resident((tm, tm)), _resident((tm, tm))]
        args += [perm, (perm.T.astype(F32) * _value_scale(tm)[None, :]).astype(BF16)]
        out_specs[8] = pl.BlockSpec((None, D_MODEL, tm), lambda i: (i, 0, 0))
        out_shape[8] = jax.ShapeDtypeStruct((n // tm, D_MODEL, tm), BF16)
    return pl.pallas_call(
        functools.partial(_proj_kernel, key_major=key_major),
        grid=(n // tm,),
        in_specs=in_specs,
        out_specs=out_specs,
        out_shape=out_shape,
        compiler_params=pltpu.CompilerParams(
            dimension_semantics=("arbitrary",), vmem_limit_bytes=VMEM_LIMIT_BYTES),
        name="proj",
    )(*args)


def _retention_kernel(*refs, n_sub, block, heads, has_state):
    if has_state:
        gl_ref, q_ref, k_ref, v_ref, g1_ref, dec_ref, qd_ref, kd_ref, rnw_ref, s0_ref = refs[:10]
        rest = refs[10:]
    else:
        gl_ref, q_ref, k_ref, v_ref, g1_ref, dec_ref, qd_ref, kd_ref, rnw_ref = refs[:9]
        s0_ref = None
        rest = refs[9:]
    r_ref, sout_ref, s_ref = rest
    first_head = pl.program_id(1) * heads
    t = pl.program_id(2)

    @pl.when(t == 0)
    def _():
        if has_state:
            s_ref[...] = s0_ref[...]
        else:
            s_ref[...] = jnp.zeros_like(s_ref)

    hs = range(heads)
    cols = [slice(j * RET_QK_DIM, (j + 1) * RET_QK_DIM) for j in hs]
    for sb in range(n_sub):
        rows = slice(sb * block, (sb + 1) * block)
        q = [q_ref[rows, c] for c in cols]
        k = [k_ref[rows, c] for c in cols]
        v = [v_ref[rows, c] for c in cols]
        s_prev = [s_ref[j] for j in hs]
        scores = [(_nt_dot(q[j], k[j]) * dec_ref[j]).astype(BF16) for j in hs]
        cross = [_dot(q[j], s_prev[j].astype(BF16)) * qd_ref[j] for j in hs]
        k_dec = [(k[j].astype(F32) * kd_ref[j]).astype(BF16) for j in hs]
        for j in hs:
            s_ref[j] = gl_ref[first_head + j] * s_prev[j] + _tn_dot(k_dec[j], v[j])
        for j in hs:
            r = _rms(_dot(scores[j], v[j]) + cross[j]) * rnw_ref[j]
            r_ref[rows, cols[j]] = (r * g1_ref[rows, cols[j]].astype(F32)).astype(BF16)

    @pl.when(t == pl.num_programs(2) - 1)
    def _():
        sout_ref[...] = s_ref[...]


def _retention_tables(block):
    log_gamma = jnp.log1p(-jnp.exp2(-5.0 - jnp.arange(RET_HEADS, dtype=F32)))
    idx = jnp.arange(block, dtype=F32)
    dist = jnp.abs(idx[:, None] - idx[None, :])
    chunk = jnp.arange(block, dtype=jnp.int32) // CHUNK
    visible = chunk[None, :] <= chunk[:, None]
    dec = jnp.where(visible[None], jnp.exp(log_gamma[:, None, None] * dist[None]), 0.0)
    qd = jnp.exp(log_gamma[:, None] * (idx[None, :] + 1.0))
    kd = jnp.exp(log_gamma[:, None] * (block - 1.0 - idx[None, :]))
    widen = lambda a: jnp.broadcast_to(a[:, :, None], (RET_HEADS, block, RET_QK_DIM))
    gl = jnp.exp(log_gamma * block)
    return gl, dec, widen(qd), widen(kd)


def _retention(rq, rk, rv, g1, ret_norm_w, state0, batch, frames):
    block = min(RET_BLOCK, frames)
    step = min(RET_STEP, frames)
    assert frames % step == 0 and step % block == 0
    assert block % CHUNK == 0 or block < CHUNK
    n_t = frames // step
    gl, dec, qd, kd = _retention_tables(block)
    has_state = state0 is not None
    heads = RET_HEADS_PER_SHORT_STEP if step == block else RET_HEADS_PER_STEP
    assert RET_HEADS % heads == 0
    tile = pl.BlockSpec((step, heads * RET_QK_DIM), lambda b, h, t: (b * n_t + t, h))
    per_head = lambda rows, cols: pl.BlockSpec((heads, rows, cols), lambda b, h, t: (h, 0, 0))
    state_spec = pl.BlockSpec((None, heads, RET_QK_DIM, RET_V_DIM), lambda b, h, t: (b, h, 0, 0))
    in_specs = [pl.BlockSpec(memory_space=pltpu.SMEM), tile, tile, tile, tile,
                per_head(block, block), per_head(block, RET_QK_DIM), per_head(block, RET_QK_DIM),
                per_head(1, RET_V_DIM)]
    args = [gl, rq, rk, rv, g1, dec, qd, kd, ret_norm_w.reshape(RET_HEADS, 1, RET_V_DIM)]
    if has_state:
        in_specs.append(state_spec)
        args.append(state0)
    return pl.pallas_call(
        functools.partial(_retention_kernel, n_sub=step // block, block=block, heads=heads,
                          has_state=has_state),
        grid=(batch, RET_HEADS // heads, n_t),
        in_specs=in_specs,
        out_specs=[tile, state_spec],
        out_shape=[jax.ShapeDtypeStruct(rq.shape, BF16),
                   jax.ShapeDtypeStruct((batch, RET_HEADS, RET_QK_DIM, RET_V_DIM), F32)],
        scratch_shapes=[pltpu.VMEM((heads, RET_QK_DIM, RET_V_DIM), F32)],
        compiler_params=pltpu.CompilerParams(
            dimension_semantics=("arbitrary", "arbitrary", "arbitrary"),
            vmem_limit_bytes=VMEM_LIMIT_BYTES),
        name="retention",
    )(*args)


def _suffix_matrix(n):
    idx = jnp.arange(n)
    m = (idx[:, None] > idx[None, :]).astype(BF16)
    return jnp.concatenate([m, m], axis=0)


def _sb_gates(z, mask):
    neg_abs = pltpu.bitcast(pltpu.bitcast(z, jnp.uint32) | jnp.uint32(0x80000000), F32)
    soft = jnp.log2(1.0 + jnp.exp2(neg_abs))
    log_beta = jnp.minimum(z, 0.0) - soft
    u = log_beta - z
    if mask is not None:
        u = jnp.where(mask, u, 0.0)
        log_beta = jnp.where(mask, log_beta, MASKED_LOG_BETA)
    u_hi = u.astype(BF16)
    u_lo = (u - u_hi.astype(F32)).astype(BF16)
    return log_beta, jnp.concatenate([u_hi, u_lo], axis=1), jnp.sum(u, axis=-1, keepdims=True)


def _sb_weights(log_beta, after, carry):
    return jnp.exp2(log_beta + carry + after).astype(BF16)


def _sb_key_run(q, ks, vs, tri2, carry, mask):
    n = len(ks)
    z = [_nt_dot(q, ks[i]) for i in range(n)]
    gates = [_sb_gates(z[i], mask) for i in range(n)]
    after = [_dot(gates[i][1], tri2) for i in range(n)]
    out = None
    for i in range(n):
        o = _dot(_sb_weights(gates[i][0], after[i], carry), vs[i])
        out = o if out is None else out + o
        carry = carry + gates[i][2]
    return out, carry


def _sb_doubled_gates(half_z, visible):
    t = jnp.tanh(half_z)
    if visible is not None:
        t = jnp.where(visible, t, -1.0)
    return 1.0 + t, 1.0 - t


def _sb_run_scan(half_z, visible):
    groups = half_z.shape[0] // SUBLANES
    pieces = [None] * groups
    totals = []
    for first in range(0, groups, SCAN_SEGMENT):
        run = None
        for r in reversed(range(first, first + SCAN_SEGMENT)):
            rows = slice(r * SUBLANES, (r + 1) * SUBLANES)
            beta2, rest2 = _sb_doubled_gates(half_z[rows], None if visible is None else visible[rows])
            pieces[r] = beta2 if run is None else beta2 * run
            run = rest2 if run is None else run * rest2
        totals.append(run * 2.0 ** -SCAN_SEGMENT)
    return pieces, jnp.concatenate(totals, axis=0)


def _value_scale(n):
    place = (jnp.arange(n) // SUBLANES) % SCAN_SEGMENT
    return jnp.exp2((place - SCAN_SEGMENT).astype(F32))


def _suffix_over_sublanes(t):
    s = lax.broadcasted_iota(jnp.int32, t.shape, 0)
    y = t
    for k in (1, 2, 4):
        below = pltpu.roll(y, SUBLANES - k, axis=0)
        y = y * jnp.where(s + k < SUBLANES, below, 1.0)
    exclusive = jnp.where(s + 1 < SUBLANES, pltpu.roll(y, SUBLANES - 1, axis=0), 1.0)
    return exclusive, jnp.broadcast_to(y[0:1, :], t.shape)


def _sb_finish(pieces, totals, v_t, carry):
    segments = totals.shape[0] // SUBLANES
    seg_totals = [totals[j * SUBLANES:(j + 1) * SUBLANES] for j in range(segments)]
    later_segments = [None] * segments
    run_total = None
    for j in reversed(range(segments)):
        later_segments[j] = run_total
        run_total = seg_totals[j] if run_total is None else run_total * seg_totals[j]
    later_runs, total = _suffix_over_sublanes(run_total)
    base = later_runs * carry
    slab = 2 * SUBLANES
    slabs_per_segment = pieces.shape[0] // slab // segments
    w = []
    for j in range(segments):
        factor = base if later_segments[j] is None else base * later_segments[j]
        factor = jnp.concatenate([factor, factor], axis=0).astype(BF16)
        for g in range(j * slabs_per_segment, (j + 1) * slabs_per_segment):
            w.append(pieces[g * slab:(g + 1) * slab] * factor)
    return _dot(v_t, jnp.concatenate(w, axis=0)), carry * total


def _sb_prompt_kernel(q_ref, k_ref, vt_ref, o_ref, acc_ref, carry_ref, z_ref, pieces_ref, totals_ref,
                      *, blk):
    i = pl.program_id(1)
    heads = SB_HEADS
    cols = [slice(hd * SB_HEAD_DIM, (hd + 1) * SB_HEAD_DIM) for hd in range(heads)]
    parked = (heads - 2, heads - 1)

    def logits(block, hd):
        start = pl.multiple_of(block * blk, blk)
        return _nt_dot(k_ref[pl.ds(start, blk), cols[hd]], q_ref[:, cols[hd]])

    def finish(pieces, totals, block, hd):
        out, carry = _sb_finish(pieces, totals, vt_ref[block, cols[hd], :], carry_ref[hd])
        acc_ref[hd] += out
        carry_ref[hd] = carry

    def key_block(block, visible, have_parked):
        z_queue = [z_ref[ahead] for ahead in range(LOGITS_AHEAD)]
        scans = {}
        for hd in range(heads):
            ahead = [hd + LOGITS_AHEAD] if hd + LOGITS_AHEAD < heads else []
            if hd == heads - LOGITS_AHEAD - 1:
                ahead += range(heads, heads + LOGITS_AHEAD)
            for nxt in ahead:
                z_queue.append(logits(block, nxt) if nxt < heads
                               else logits(jnp.maximum(block - 1, 0), nxt - heads))
            if hd < len(parked):
                if have_parked:
                    finish(pieces_ref[hd], totals_ref[hd], block + 1, parked[hd])
            else:
                finish(*scans.pop(hd - len(parked)), block, hd - len(parked))
            pieces, totals = _sb_run_scan(z_queue.pop(0), visible)
            scans[hd] = (jnp.concatenate(pieces, axis=0).astype(BF16), totals)
        for slot, hd in enumerate(parked):
            pieces_ref[slot], totals_ref[slot] = scans[hd]
        for ahead in range(LOGITS_AHEAD):
            z_ref[ahead] = z_queue[ahead]

    acc_ref[...] = jnp.zeros_like(acc_ref)
    carry_ref[...] = jnp.ones_like(carry_ref)
    for ahead in range(LOGITS_AHEAD):
        z_ref[ahead] = logits(i, ahead)
    row = lax.broadcasted_iota(jnp.int32, (blk, blk), 0)
    key = (row & (SUBLANES - 1)) * (blk // SUBLANES) + (row >> (SUBLANES.bit_length() - 1))
    key_block(i, key < lax.broadcasted_iota(jnp.int32, (blk, blk), 1), False)

    @pl.when(i % 2 == 1)
    def _():
        key_block(i - 1, None, True)

    def trip(t, _):
        top = i - 1 - i % 2 - 2 * t
        key_block(top, None, True)
        key_block(top - 1, None, True)
        return 0

    lax.fori_loop(0, i // 2, trip, 0)
    for slot, hd in enumerate(parked):
        finish(pieces_ref[slot], totals_ref[slot], 0, hd)
    for hd in range(heads):
        o_ref[:, cols[hd]] = acc_ref[hd].T.astype(o_ref.dtype)


def _sb_prompt(sq, sk_km, sv_t, batch, frames):
    blk = SB_BLOCK
    assert frames % blk == 0
    n_q = frames // blk
    q_tile = pl.BlockSpec((blk, D_MODEL), lambda b, i: (b * n_q + i, 0))
    k_all = pl.BlockSpec((frames, D_MODEL), lambda b, i: (b, 0))
    vt_all = pl.BlockSpec((None, n_q, D_MODEL, blk), lambda b, i: (b, 0, 0, 0))
    return pl.pallas_call(
        functools.partial(_sb_prompt_kernel, blk=blk),
        grid=(batch, n_q),
        in_specs=[q_tile, k_all, vt_all],
        out_specs=q_tile,
        out_shape=jax.ShapeDtypeStruct(sq.shape, BF16),
        scratch_shapes=[pltpu.VMEM((SB_HEADS, SB_HEAD_DIM, blk), F32),
                        pltpu.VMEM((SB_HEADS, SUBLANES, blk), F32),
                        pltpu.VMEM((LOGITS_AHEAD, blk, blk), F32),
                        pltpu.VMEM((2, blk, blk), BF16),
                        pltpu.VMEM((2, blk // SCAN_SEGMENT, blk), F32)],
        compiler_params=pltpu.CompilerParams(
            dimension_semantics=("arbitrary", "arbitrary"), vmem_limit_bytes=VMEM_LIMIT_BYTES),
        name="sb_prompt",
    )(sq, sk_km, sv_t.reshape(batch, n_q, D_MODEL, blk))


def _sb_decode_kernel(q_ref, kn_ref, vn_ref, kc_ref, vc_ref, tri_ref, o_ref,
                      qbd_ref, acc_ref, carry_ref, *, new, key_tile, bk):
    kt = pl.program_id(1)
    rows = SB_HEADS * new
    tri2 = tri_ref[...]

    @pl.when(kt == 0)
    def _():
        q = q_ref[...]
        col = lax.broadcasted_iota(jnp.int32, (new, D_MODEL), 1)
        for hd in range(SB_HEADS):
            keep = (col >= hd * SB_HEAD_DIM) & (col < (hd + 1) * SB_HEAD_DIM)
            qbd_ref[hd * new:(hd + 1) * new, :] = jnp.where(keep, q, jnp.zeros_like(q))
        pad = kn_ref.shape[0]
        l_q = jnp.concatenate([lax.broadcasted_iota(jnp.int32, (new, pad), 0)] * SB_HEADS, axis=0)
        l_k = lax.broadcasted_iota(jnp.int32, (rows, pad), 1)
        tri2_pad = jnp.concatenate([tri2[:pad, :pad], tri2[bk:bk + pad, :pad]], axis=0)
        out, carry = _sb_key_run(qbd_ref[...], [kn_ref[...]], [vn_ref[...]], tri2_pad,
                                 jnp.zeros((rows, 1), F32), l_k < l_q)
        acc_ref[...] = out
        carry_ref[...] = carry

    def head_major(ref, sb):
        base = sb * bk * SB_HEADS
        return jnp.concatenate(
            [ref[pl.ds(base + hd, bk, stride=SB_HEADS), :].astype(BF16) for hd in range(SB_HEADS)], axis=1)

    order = list(reversed(range(key_tile // bk)))
    out, carry = _sb_key_run(qbd_ref[...], [head_major(kc_ref, sb) for sb in order],
                             [head_major(vc_ref, sb) for sb in order], tri2, carry_ref[...], None)
    acc_ref[...] += out
    carry_ref[...] = carry

    @pl.when(kt == pl.num_programs(1) - 1)
    def _():
        for hd in range(SB_HEADS):
            cols = slice(hd * SB_HEAD_DIM, (hd + 1) * SB_HEAD_DIM)
            o_ref[:, cols] = acc_ref[hd * new:(hd + 1) * new, cols].astype(o_ref.dtype)


def _sb_decode(sq, sk, sv, cache_k, cache_v, batch, new):
    past = cache_k.shape[1] // SB_HEADS
    key_tile = min(DECODE_KEY_TILE, past)
    bk = min(SB_BLOCK, key_tile)
    assert past % key_tile == 0 and key_tile % bk == 0 and new <= DECODE_NEW_PAD <= bk
    n_kt = past // key_tile
    pad = DECODE_NEW_PAD
    pad_keys = lambda a: jnp.pad(a.reshape(batch, new, D_MODEL), ((0, 0), (0, pad - new), (0, 0)))
    rows = SB_HEADS * new
    q_tile = pl.BlockSpec((new, D_MODEL), lambda b, kt: (b, 0))
    new_tile = pl.BlockSpec((None, pad, D_MODEL), lambda b, kt: (b, 0, 0))
    cache_tile = pl.BlockSpec((None, key_tile * SB_HEADS, SB_HEAD_DIM),
                              lambda b, kt: (b, n_kt - 1 - kt, 0))
    return pl.pallas_call(
        functools.partial(_sb_decode_kernel, new=new, key_tile=key_tile, bk=bk),
        grid=(batch, n_kt),
        in_specs=[q_tile, new_tile, new_tile, cache_tile, cache_tile, _resident((2 * bk, bk))],
        out_specs=q_tile,
        out_shape=jax.ShapeDtypeStruct(sq.shape, BF16),
        scratch_shapes=[pltpu.VMEM((rows, D_MODEL), BF16), pltpu.VMEM((rows, D_MODEL), F32),
                        pltpu.VMEM((rows, 1), F32)],
        compiler_params=pltpu.CompilerParams(
            dimension_semantics=("arbitrary", "arbitrary"), vmem_limit_bytes=VMEM_LIMIT_BYTES),
        name="sb_decode",
    )(sq, pad_keys(sk), pad_keys(sv), cache_k, cache_v, _suffix_matrix(bk))


def _out_mlp_kernel(x_ref, r_ref, s_ref, g2_ref, wo_ref, nw_ref, wu_ref, wd_ref, y_ref, *, chains):
    n = x_ref.shape[0] // chains
    rows = [slice(c * n, (c + 1) * n) for c in range(chains)]
    mix = [(r_ref[rw, :].astype(F32) + g2_ref[rw, :].astype(F32) * s_ref[rw, :].astype(F32)).astype(BF16)
           for rw in rows]
    x1 = [x_ref[rw, :] + _dot(m, wo_ref[...]) for rw, m in zip(rows, mix)]
    h = [(_rms(v) * nw_ref[...]).astype(BF16) for v in x1]
    up = [jnp.maximum(_dot(v, wu_ref[...]), 0.0) for v in h]
    act = [(u * u).astype(BF16) for u in up]
    for rw, v, a in zip(rows, x1, act):
        y_ref[rw, :] = v + _dot(a, wd_ref[...])


def _out_mlp(x, r, s, g2, w_out_bf, norm_w, w_up_bf, w_down_bf):
    n = x.shape[0]
    chains = OUT_MLP_CHAINS if n % (OUT_MLP_CHAINS * TOKEN_TILE) == 0 else 1
    tm = chains * TOKEN_TILE
    assert n % tm == 0
    tile = pl.BlockSpec((tm, D_MODEL), lambda i: (i, 0))
    return pl.pallas_call(
        functools.partial(_out_mlp_kernel, chains=chains),
        grid=(n // tm,),
        in_specs=[tile, tile, tile, tile, _resident((D_MODEL, D_MODEL)), _resident((1, D_MODEL)),
                  _resident((D_MODEL, D_FF)), _resident((D_FF, D_MODEL))],
        out_specs=tile,
        out_shape=jax.ShapeDtypeStruct((n, D_MODEL), F32),
        compiler_params=pltpu.CompilerParams(
            dimension_semantics=("arbitrary",), vmem_limit_bytes=VMEM_LIMIT_BYTES),
        name="out_mlp",
    )(x, r, s, g2, w_out_bf, norm_w, w_up_bf, w_down_bf)


def _rope_tables(pos):
    half = RET_QK_DIM // 2
    inv_freq = ROPE_BASE ** (-jnp.arange(half, dtype=F32) / half)
    ang = pos.astype(F32)[:, None] * inv_freq[None, :]
    return jnp.cos(ang), jnp.sin(ang)


def kernel(x_prompt, x_sample, cache_sb_k, cache_sb_v, state_ret, norm_mix_w, w_in, ret_norm_w,
           sb_q_norm_w, sb_k_norm_w, w_out, norm_mlp_w, w_up, w_down):
    depth = w_in.shape[0]
    batch, frames, _ = x_prompt.shape
    dec_batch, new, _ = x_sample.shape
    past = cache_sb_k.shape[2]

    cos_p, sin_p = _rope_tables(jnp.arange(frames, dtype=jnp.int32))
    cos_s, sin_s = _rope_tables(past + jnp.arange(new, dtype=jnp.int32))
    reps = max(TOKEN_TILE // new, 1)
    cos_s, sin_s = jnp.tile(cos_s, (reps, 1)), jnp.tile(sin_s, (reps, 1))

    xp = x_prompt.reshape(batch * frames, D_MODEL)
    xs = x_sample.reshape(dec_batch * new, D_MODEL)
    outs = [[] for _ in range(6)]
    for l in range(depth):
        w_in_bf = w_in[l].astype(BF16)
        w_out_bf = w_out[l].astype(BF16)
        w_up_bf = w_up[l].astype(BF16)
        w_down_bf = w_down[l].astype(BF16)
        nmix = norm_mix_w[l].reshape(1, D_MODEL)
        nmlp = norm_mlp_w[l].reshape(1, D_MODEL)
        qnw = sb_q_norm_w[l].reshape(1, SB_HEAD_DIM)
        knw = sb_k_norm_w[l].reshape(1, SB_HEAD_DIM)

        rq, rk, rv, g1, sq, sk, skb, sv, svb, g2 = _project(xp, nmix, w_in_bf, cos_p, sin_p, qnw, knw, True)
        r, state_p = _retention(rq, rk, rv, g1, ret_norm_w[l], None, batch, frames)
        o_sb = _sb_prompt(sq, skb, svb, batch, frames)
        xp = _out_mlp(xp, r, o_sb, g2, w_out_bf, nmlp, w_up_bf, w_down_bf)
        outs[0].append(sk.reshape(batch, frames, SB_HEADS, SB_HEAD_DIM))
        outs[1].append(sv.reshape(batch, frames, SB_HEADS, SB_HEAD_DIM))
        outs[2].append(state_p.astype(state_ret.dtype))

        rq, rk, rv, g1, sq, sk, skb, sv, svb, g2 = _project(xs, nmix, w_in_bf, cos_s, sin_s, qnw, knw, False)
        r, state_s = _retention(rq, rk, rv, g1, ret_norm_w[l], state_ret[l].astype(F32), dec_batch, new)
        o_sb = _sb_decode(sq, skb, svb,
                          cache_sb_k[l].reshape(dec_batch, past * SB_HEADS, SB_HEAD_DIM),
                          cache_sb_v[l].reshape(dec_batch, past * SB_HEADS, SB_HEAD_DIM), dec_batch, new)
        xs = _out_mlp(xs, r, o_sb, g2, w_out_bf, nmlp, w_up_bf, w_down_bf)
        outs[3].append(sk.reshape(dec_batch, new, SB_HEADS, SB_HEAD_DIM))
        outs[4].append(sv.reshape(dec_batch, new, SB_HEADS, SB_HEAD_DIM))
        outs[5].append(state_s.astype(state_ret.dtype))

    return (xp.reshape(batch, frames, D_MODEL), xs.reshape(dec_batch, new, D_MODEL),
            jnp.stack(outs[0]), jnp.stack(outs[1]), jnp.stack(outs[2]),
            jnp.stack(outs[3]), jnp.stack(outs[4]), jnp.stack(outs[5]))
```

```python
import functools
import math

import jax
import jax.numpy as jnp
from jax import lax
from jax.experimental import pallas as pl
from jax.experimental.pallas import tpu as pltpu

D_MODEL = 1024
CHUNK = 64
RET_HEADS = 4
RET_QK_DIM = 256
RET_V_DIM = D_MODEL // RET_HEADS
SB_HEADS = 8
SB_HEAD_DIM = D_MODEL // SB_HEADS
D_FF = 4 * D_MODEL
ROPE_BASE = 10000.0
EPS = 1e-6
LOG2_E = math.log2(math.e)
N_GROUPS = 9

F32 = jnp.float32
BF16 = jnp.bfloat16

VMEM_LIMIT_BYTES = 56 * 1024 * 1024
TOKEN_TILE = 256
PROJ_CHAINS = 2
OUT_MLP_CHAINS = 2
RET_BLOCK = 256
RET_STEP = 1024
RET_HEADS_PER_STEP = 2
RET_HEADS_PER_SHORT_STEP = 4
SB_BLOCK = 256
SCAN_SEGMENT = 8
LOGITS_AHEAD = 2
SUBLANES = 8
SB_PROMPT_QUERY_SCALE = SB_HEAD_DIM ** -0.5 * 0.5
SB_DECODE_QUERY_SCALE = SB_HEAD_DIM ** -0.5 * LOG2_E
MASKED_LOG_BETA = -1e30
DECODE_KEY_TILE = 1024
DECODE_NEW_PAD = 128


def _nt_dot(a, b):
    return lax.dot_general(a, b, (((1,), (1,)), ((), ())), preferred_element_type=F32)


def _tn_dot(a, b):
    return lax.dot_general(a, b, (((0,), (0,)), ((), ())), preferred_element_type=F32)


def _dot(a, b):
    return jnp.dot(a, b, preferred_element_type=F32)


def _sigmoid(x):
    return 1.0 / (1.0 + jnp.exp(-x))


def _rms(x):
    return x * lax.rsqrt(jnp.mean(x * x, axis=-1, keepdims=True) + EPS)


def _resident(shape):
    return pl.BlockSpec(shape, lambda *_: (0,) * len(shape), pipeline_mode=pl.Buffered(1))


def _proj_kernel(*refs, key_major):
    if key_major:
        (x_ref, nw_ref, w_ref, cos_ref, sin_ref, qnw_ref, knw_ref, perm_ref, perm_t_ref,
         rq_ref, rk_ref, rv_ref, g1_ref, sq_ref, skf_ref, skb_ref, svf_ref, svb_ref, g2_ref) = refs
    else:
        (x_ref, nw_ref, w_ref, cos_ref, sin_ref, qnw_ref, knw_ref,
         rq_ref, rk_ref, rv_ref, g1_ref, sq_ref, skf_ref, skb_ref, svf_ref, svb_ref, g2_ref) = refs
    chains = x_ref.shape[0] // TOKEN_TILE
    blocks = [slice(c * TOKEN_TILE, (c + 1) * TOKEN_TILE) for c in range(chains)]
    h = [(_rms(x_ref[rw, :]) * nw_ref[...]).astype(BF16) for rw in blocks]

    def group(g):
        return [_dot(hc, w_ref[:, g * D_MODEL:(g + 1) * D_MODEL]) for hc in h]

    half = RET_QK_DIM // 2

    def rotary(p, out_ref, rw, scale):
        cos = cos_ref[rw, :]
        sin = sin_ref[rw, :]
        for hd in range(RET_HEADS):
            lo = hd * RET_QK_DIM
            x1 = p[:, lo:lo + half]
            x2 = p[:, lo + half:lo + 2 * half]
            o1 = x1 * cos - x2 * sin
            o2 = x1 * sin + x2 * cos
            if scale != 1.0:
                o1 = o1 * scale
                o2 = o2 * scale
            out_ref[rw, lo:lo + half] = o1.astype(out_ref.dtype)
            out_ref[rw, lo + half:lo + 2 * half] = o2.astype(out_ref.dtype)

    def head_norm(p, w, scale=1.0):
        ys = []
        for hd in range(SB_HEADS):
            y = _rms(p[:, hd * SB_HEAD_DIM:(hd + 1) * SB_HEAD_DIM]) * w
            ys.append(y if scale == 1.0 else y * scale)
        return jnp.concatenate(ys, axis=1)

    for rw, p in zip(blocks, group(0)):
        rotary(p, rq_ref, rw, 1.0)
    for rw, p in zip(blocks, group(1)):
        rotary(p, rk_ref, rw, RET_QK_DIM ** -0.5)
    for rw, p in zip(blocks, group(2)):
        rv_ref[rw, :] = p.astype(BF16)
    rg = group(3)
    query_scale = SB_PROMPT_QUERY_SCALE if key_major else SB_DECODE_QUERY_SCALE
    for rw, p in zip(blocks, group(4)):
        sq_ref[rw, :] = head_norm(p, qnw_ref[...], query_scale).astype(BF16)
    for c, (rw, p) in enumerate(zip(blocks, group(5))):
        sk = head_norm(p, knw_ref[...])
        skf_ref[rw, :] = sk
        skb_ref[rw, :] = (_dot(perm_ref[...], sk.astype(BF16)) if key_major else sk).astype(BF16)
    for c, (rw, sv) in enumerate(zip(blocks, group(6))):
        svf_ref[rw, :] = sv
        if key_major:
            svb_ref[c] = _tn_dot(sv.astype(BF16), perm_t_ref[...]).astype(BF16)
        else:
            svb_ref[rw, :] = sv.astype(BF16)
    for rw, r, a in zip(blocks, rg, group(7)):
        g1_ref[rw, :] = (r * _sigmoid(r) * _sigmoid(a)).astype(BF16)
    for rw, p in zip(blocks, group(8)):
        g2_ref[rw, :] = _sigmoid(p).astype(BF16)


def _key_order(n):
    row = jnp.arange(n)
    key = (row % SUBLANES) * (n // SUBLANES) + row // SUBLANES
    return (key[:, None] == jnp.arange(n)[None, :]).astype(BF16)


def _project(x, norm_w, w_in_bf, cos_tab, sin_tab, qnw, knw, key_major):
    n = x.shape[0]
    tm = PROJ_CHAINS * TOKEN_TILE
    assert n % tm == 0 and cos_tab.shape[0] % tm == 0
    pos_blocks = cos_tab.shape[0] // tm
    tile = pl.BlockSpec((tm, D_MODEL), lambda i: (i, 0))
    pos_tile = pl.BlockSpec((tm, RET_QK_DIM // 2), lambda i: (i % pos_blocks, 0))
    bf = jax.ShapeDtypeStruct((n, D_MODEL), BF16)
    f32 = jax.ShapeDtypeStruct((n, D_MODEL), F32)
    in_specs = [tile, _resident((1, D_MODEL)), _resident((D_MODEL, N_GROUPS * D_MODEL)),
                pos_tile, pos_tile, _resident((1, SB_HEAD_DIM)), _resident((1, SB_HEAD_DIM))]
    args = [x, norm_w, w_in_bf, cos_tab, sin_tab, qnw, knw]
    out_specs = [tile] * 10
    out_shape = [bf, bf, bf, bf, bf, f32, bf, f32, bf, bf]
    if key_major:
        blk = TOKEN_TILE
        assert blk == SB_BLOCK
        perm = _key_order(blk)
        in_specs += [_resident((blk, blk)), _resident((blk, blk))]
        args += [perm, (perm.T.astype(F32) * _value_scale(blk)[None, :]).astype(BF16)]
        out_specs[8] = pl.BlockSpec((PROJ_CHAINS, D_MODEL, blk), lambda i: (i, 0, 0))
        out_shape[8] = jax.ShapeDtypeStruct((n // blk, D_MODEL, blk), BF16)
    return pl.pallas_call(
        functools.partial(_proj_kernel, key_major=key_major),
        grid=(n // tm,),
        in_specs=in_specs,
        out_specs=out_specs,
        out_shape=out_shape,
        compiler_params=pltpu.CompilerParams(
            dimension_semantics=("arbitrary",), vmem_limit_bytes=VMEM_LIMIT_BYTES),
        name="proj",
    )(*args)


def _retention_kernel(*refs, n_sub, block, heads, has_state):
    if has_state:
        gl_ref, q_ref, k_ref, v_ref, g1_ref, dec_ref, qd_ref, kd_ref, rnw_ref, s0_ref = refs[:10]
        rest = refs[10:]
    else:
        gl_ref, q_ref, k_ref, v_ref, g1_ref, dec_ref, qd_ref, kd_ref, rnw_ref = refs[:9]
        s0_ref = None
        rest = refs[9:]
    r_ref, sout_ref, s_ref = rest
    first_head = pl.program_id(1) * heads
    t = pl.program_id(2)

    @pl.when(t == 0)
    def _():
        if has_state:
            s_ref[...] = s0_ref[...]
        else:
            s_ref[...] = jnp.zeros_like(s_ref)

    hs = range(heads)
    cols = [slice(j * RET_QK_DIM, (j + 1) * RET_QK_DIM) for j in hs]
    for sb in range(n_sub):
        rows = slice(sb * block, (sb + 1) * block)
        q = [q_ref[rows, c] for c in cols]
        k = [k_ref[rows, c] for c in cols]
        v = [v_ref[rows, c] for c in cols]
        s_prev = [s_ref[j] for j in hs]
        scores = [(_nt_dot(q[j], k[j]) * dec_ref[j]).astype(BF16) for j in hs]
        cross = [_dot(q[j], s_prev[j].astype(BF16)) * qd_ref[j] for j in hs]
        k_dec = [(k[j].astype(F32) * kd_ref[j]).astype(BF16) for j in hs]
        for j in hs:
            s_ref[j] = gl_ref[first_head + j] * s_prev[j] + _tn_dot(k_dec[j], v[j])
        for j in hs:
            r = _rms(_dot(scores[j], v[j]) + cross[j]) * rnw_ref[j]
            r_ref[rows, cols[j]] = (r * g1_ref[rows, cols[j]].astype(F32)).astype(BF16)

    @pl.when(t == pl.num_programs(2) - 1)
    def _():
        sout_ref[...] = s_ref[...]


def _retention_tables(block):
    log_gamma = jnp.log1p(-jnp.exp2(-5.0 - jnp.arange(RET_HEADS, dtype=F32)))
    idx = jnp.arange(block, dtype=F32)
    dist = jnp.abs(idx[:, None] - idx[None, :])
    chunk = jnp.arange(block, dtype=jnp.int32) // CHUNK
    visible = chunk[None, :] <= chunk[:, None]
    dec = jnp.where(visible[None], jnp.exp(log_gamma[:, None, None] * dist[None]), 0.0)
    qd = jnp.exp(log_gamma[:, None] * (idx[None, :] + 1.0))
    kd = jnp.exp(log_gamma[:, None] * (block - 1.0 - idx[None, :]))
    widen = lambda a: jnp.broadcast_to(a[:, :, None], (RET_HEADS, block, RET_QK_DIM))
    gl = jnp.exp(log_gamma * block)
    return gl, dec, widen(qd), widen(kd)


def _retention(rq, rk, rv, g1, ret_norm_w, state0, batch, frames):
    block = min(RET_BLOCK, frames)
    step = min(RET_STEP, frames)
    assert frames % step == 0 and step % block == 0
    assert block % CHUNK == 0 or block < CHUNK
    n_t = frames // step
    gl, dec, qd, kd = _retention_tables(block)
    has_state = state0 is not None
    heads = RET_HEADS_PER_SHORT_STEP if step == block else RET_HEADS_PER_STEP
    assert RET_HEADS % heads == 0
    tile = pl.BlockSpec((step, heads * RET_QK_DIM), lambda b, h, t: (b * n_t + t, h))
    per_head = lambda rows, cols: pl.BlockSpec((heads, rows, cols), lambda b, h, t: (h, 0, 0))
    state_spec = pl.BlockSpec((None, heads, RET_QK_DIM, RET_V_DIM), lambda b, h, t: (b, h, 0, 0))
    in_specs = [pl.BlockSpec(memory_space=pltpu.SMEM), tile, tile, tile, tile,
                per_head(block, block), per_head(block, RET_QK_DIM), per_head(block, RET_QK_DIM),
                per_head(1, RET_V_DIM)]
    args = [gl, rq, rk, rv, g1, dec, qd, kd, ret_norm_w.reshape(RET_HEADS, 1, RET_V_DIM)]
    if has_state:
        in_specs.append(state_spec)
        args.append(state0)
    return pl.pallas_call(
        functools.partial(_retention_kernel, n_sub=step // block, block=block, heads=heads,
                          has_state=has_state),
        grid=(batch, RET_HEADS // heads, n_t),
        in_specs=in_specs,
        out_specs=[tile, state_spec],
        out_shape=[jax.ShapeDtypeStruct(rq.shape, BF16),
                   jax.ShapeDtypeStruct((batch, RET_HEADS, RET_QK_DIM, RET_V_DIM), F32)],
        scratch_shapes=[pltpu.VMEM((heads, RET_QK_DIM, RET_V_DIM), F32)],
        compiler_params=pltpu.CompilerParams(
            dimension_semantics=("arbitrary", "arbitrary", "arbitrary"),
            vmem_limit_bytes=VMEM_LIMIT_BYTES),
        name="retention",
    )(*args)


def _suffix_matrix(n):
    idx = jnp.arange(n)
    m = (idx[:, None] > idx[None, :]).astype(BF16)
    return jnp.concatenate([m, m], axis=0)


def _sb_gates(z, mask):
    neg_abs = pltpu.bitcast(pltpu.bitcast(z, jnp.uint32) | jnp.uint32(0x80000000), F32)
    soft = jnp.log2(1.0 + jnp.exp2(neg_abs))
    log_beta = jnp.minimum(z, 0.0) - soft
    u = log_beta - z
    if mask is not None:
        u = jnp.where(mask, u, 0.0)
        log_beta = jnp.where(mask, log_beta, MASKED_LOG_BETA)
    u_hi = u.astype(BF16)
    u_lo = (u - u_hi.astype(F32)).astype(BF16)
    return log_beta, jnp.concatenate([u_hi, u_lo], axis=1), jnp.sum(u, axis=-1, keepdims=True)


def _sb_weights(log_beta, after, carry):
    return jnp.exp2(log_beta + carry + after).astype(BF16)


def _sb_key_run(q, ks, vs, tri2, carry, mask):
    n = len(ks)
    z = [_nt_dot(q, ks[i]) for i in range(n)]
    gates = [_sb_gates(z[i], mask) for i in range(n)]
    after = [_dot(gates[i][1], tri2) for i in range(n)]
    out = None
    for i in range(n):
        o = _dot(_sb_weights(gates[i][0], after[i], carry), vs[i])
        out = o if out is None else out + o
        carry = carry + gates[i][2]
    return out, carry


def _sb_doubled_gates(half_z, visible):
    t = jnp.tanh(half_z)
    if visible is not None:
        t = jnp.where(visible, t, -1.0)
    return 1.0 + t, 1.0 - t


def _sb_run_scan(half_z, visible):
    groups = half_z.shape[0] // SUBLANES
    pieces = [None] * groups
    totals = []
    for first in range(0, groups, SCAN_SEGMENT):
        run = None
        for r in reversed(range(first, first + SCAN_SEGMENT)):
            rows = slice(r * SUBLANES, (r + 1) * SUBLANES)
            beta2, rest2 = _sb_doubled_gates(half_z[rows], None if visible is None else visible[rows])
            pieces[r] = beta2 if run is None else beta2 * run
            run = rest2 if run is None else run * rest2
        totals.append(run * 2.0 ** -SCAN_SEGMENT)
    return pieces, jnp.concatenate(totals, axis=0)


def _value_scale(n):
    place = (jnp.arange(n) // SUBLANES) % SCAN_SEGMENT
    return jnp.exp2((place - SCAN_SEGMENT).astype(F32))


def _suffix_over_sublanes(t):
    s = lax.broadcasted_iota(jnp.int32, t.shape, 0)
    y = t
    for k in (1, 2, 4):
        below = pltpu.roll(y, SUBLANES - k, axis=0)
        y = y * jnp.where(s + k < SUBLANES, below, 1.0)
    exclusive = jnp.where(s + 1 < SUBLANES, pltpu.roll(y, SUBLANES - 1, axis=0), 1.0)
    return exclusive, jnp.broadcast_to(y[0:1, :], t.shape)


def _sb_finish(pieces, totals, v_t, carry):
    segments = totals.shape[0] // SUBLANES
    seg_totals = [totals[j * SUBLANES:(j + 1) * SUBLANES] for j in range(segments)]
    later_segments = [None] * segments
    run_total = None
    for j in reversed(range(segments)):
        later_segments[j] = run_total
        run_total = seg_totals[j] if run_total is None else run_total * seg_totals[j]
    later_runs, total = _suffix_over_sublanes(run_total)
    base = later_runs * carry
    slab = 2 * SUBLANES
    slabs_per_segment = pieces.shape[0] // slab // segments
    w = []
    for j in range(segments):
        factor = base if later_segments[j] is None else base * later_segments[j]
        factor = jnp.concatenate([factor, factor], axis=0).astype(BF16)
        for g in range(j * slabs_per_segment, (j + 1) * slabs_per_segment):
            w.append(pieces[g * slab:(g + 1) * slab] * factor)
    return _dot(v_t, jnp.concatenate(w, axis=0)), carry * total


def _sb_prompt_kernel(q_ref, k_ref, vt_ref, o_ref, acc_ref, carry_ref, z_ref, pieces_ref, totals_ref,
                      *, blk):
    i = pl.program_id(1)
    heads = SB_HEADS
    cols = [slice(hd * SB_HEAD_DIM, (hd + 1) * SB_HEAD_DIM) for hd in range(heads)]
    parked = (heads - 2, heads - 1)

    def logits(block, hd):
        start = pl.multiple_of(block * blk, blk)
        return _nt_dot(k_ref[pl.ds(start, blk), cols[hd]], q_ref[:, cols[hd]])

    def finish(pieces, totals, block, hd):
        out, carry = _sb_finish(pieces, totals, vt_ref[block, cols[hd], :], carry_ref[hd])
        acc_ref[hd] += out
        carry_ref[hd] = carry

    def key_block(block, visible, have_parked):
        z_queue = [z_ref[ahead] for ahead in range(LOGITS_AHEAD)]
        scans = {}
        for hd in range(heads):
            ahead = [hd + LOGITS_AHEAD] if hd + LOGITS_AHEAD < heads else []
            if hd == heads - LOGITS_AHEAD - 1:
                ahead += range(heads, heads + LOGITS_AHEAD)
            for nxt in ahead:
                z_queue.append(logits(block, nxt) if nxt < heads
                               else logits(jnp.maximum(block - 1, 0), nxt - heads))
            if hd < len(parked):
                if have_parked:
                    finish(pieces_ref[hd], totals_ref[hd], block + 1, parked[hd])
            else:
                finish(*scans.pop(hd - len(parked)), block, hd - len(parked))
            pieces, totals = _sb_run_scan(z_queue.pop(0), visible)
            scans[hd] = (jnp.concatenate(pieces, axis=0).astype(BF16), totals)
        for slot, hd in enumerate(parked):
            pieces_ref[slot], totals_ref[slot] = scans[hd]
        for ahead in range(LOGITS_AHEAD):
            z_ref[ahead] = z_queue[ahead]

    acc_ref[...] = jnp.zeros_like(acc_ref)
    carry_ref[...] = jnp.ones_like(carry_ref)
    for ahead in range(LOGITS_AHEAD):
        z_ref[ahead] = logits(i, ahead)
    row = lax.broadcasted_iota(jnp.int32, (blk, blk), 0)
    key = (row & (SUBLANES - 1)) * (blk // SUBLANES) + (row >> (SUBLANES.bit_length() - 1))
    key_block(i, key < lax.broadcasted_iota(jnp.int32, (blk, blk), 1), False)

    @pl.when(i % 2 == 1)
    def _():
        key_block(i - 1, None, True)

    def trip(t, _):
        top = i - 1 - i % 2 - 2 * t
        key_block(top, None, True)
        key_block(top - 1, None, True)
        return 0

    lax.fori_loop(0, i // 2, trip, 0)
    for slot, hd in enumerate(parked):
        finish(pieces_ref[slot], totals_ref[slot], 0, hd)
    for hd in range(heads):
        o_ref[:, cols[hd]] = acc_ref[hd].T.astype(o_ref.dtype)


def _sb_prompt(sq, sk_km, sv_t, batch, frames):
    blk = SB_BLOCK
    assert frames % blk == 0
    n_q = frames // blk
    q_tile = pl.BlockSpec((blk, D_MODEL), lambda b, i: (b * n_q + i, 0))
    k_all = pl.BlockSpec((frames, D_MODEL), lambda b, i: (b, 0))
    vt_all = pl.BlockSpec((None, n_q, D_MODEL, blk), lambda b, i: (b, 0, 0, 0))
    return pl.pallas_call(
        functools.partial(_sb_prompt_kernel, blk=blk),
        grid=(batch, n_q),
        in_specs=[q_tile, k_all, vt_all],
        out_specs=q_tile,
        out_shape=jax.ShapeDtypeStruct(sq.shape, BF16),
        scratch_shapes=[pltpu.VMEM((SB_HEADS, SB_HEAD_DIM, blk), F32),
                        pltpu.VMEM((SB_HEADS, SUBLANES, blk), F32),
                        pltpu.VMEM((LOGITS_AHEAD, blk, blk), F32),
                        pltpu.VMEM((2, blk, blk), BF16),
                        pltpu.VMEM((2, blk // SCAN_SEGMENT, blk), F32)],
        compiler_params=pltpu.CompilerParams(
            dimension_semantics=("arbitrary", "arbitrary"), vmem_limit_bytes=VMEM_LIMIT_BYTES),
        name="sb_prompt",
    )(sq, sk_km, sv_t.reshape(batch, n_q, D_MODEL, blk))


def _sb_decode_kernel(q_ref, kn_ref, vn_ref, kc_ref, vc_ref, tri_ref, o_ref,
                      qbd_ref, acc_ref, carry_ref, *, new, key_tile, bk):
    kt = pl.program_id(1)
    rows = SB_HEADS * new
    tri2 = tri_ref[...]

    @pl.when(kt == 0)
    def _():
        q = q_ref[...]
        col = lax.broadcasted_iota(jnp.int32, (new, D_MODEL), 1)
        for hd in range(SB_HEADS):
            keep = (col >= hd * SB_HEAD_DIM) & (col < (hd + 1) * SB_HEAD_DIM)
            qbd_ref[hd * new:(hd + 1) * new, :] = jnp.where(keep, q, jnp.zeros_like(q))
        pad = kn_ref.shape[0]
        l_q = jnp.concatenate([lax.broadcasted_iota(jnp.int32, (new, pad), 0)] * SB_HEADS, axis=0)
        l_k = lax.broadcasted_iota(jnp.int32, (rows, pad), 1)
        tri2_pad = jnp.concatenate([tri2[:pad, :pad], tri2[bk:bk + pad, :pad]], axis=0)
        out, carry = _sb_key_run(qbd_ref[...], [kn_ref[...]], [vn_ref[...]], tri2_pad,
                                 jnp.zeros((rows, 1), F32), l_k < l_q)
        acc_ref[...] = out
        carry_ref[...] = carry

    def head_major(ref, sb):
        base = sb * bk * SB_HEADS
        return jnp.concatenate(
            [ref[pl.ds(base + hd, bk, stride=SB_HEADS), :].astype(BF16) for hd in range(SB_HEADS)], axis=1)

    order = list(reversed(range(key_tile // bk)))
    out, carry = _sb_key_run(qbd_ref[...], [head_major(kc_ref, sb) for sb in order],
                             [head_major(vc_ref, sb) for sb in order], tri2, carry_ref[...], None)
    acc_ref[...] += out
    carry_ref[...] = carry

    @pl.when(kt == pl.num_programs(1) - 1)
    def _():
        for hd in range(SB_HEADS):
            cols = slice(hd * SB_HEAD_DIM, (hd + 1) * SB_HEAD_DIM)
            o_ref[:, cols] = acc_ref[hd * new:(hd + 1) * new, cols].astype(o_ref.dtype)


def _sb_decode(sq, sk, sv, cache_k, cache_v, batch, new):
    past = cache_k.shape[1] // SB_HEADS
    key_tile = min(DECODE_KEY_TILE, past)
    bk = min(SB_BLOCK, key_tile)
    assert past % key_tile == 0 and key_tile % bk == 0 and new <= DECODE_NEW_PAD <= bk
    n_kt = past // key_tile
    pad = DECODE_NEW_PAD
    pad_keys = lambda a: jnp.pad(a.reshape(batch, new, D_MODEL), ((0, 0), (0, pad - new), (0, 0)))
    rows = SB_HEADS * new
    q_tile = pl.BlockSpec((new, D_MODEL), lambda b, kt: (b, 0))
    new_tile = pl.BlockSpec((None, pad, D_MODEL), lambda b, kt: (b, 0, 0))
    cache_tile = pl.BlockSpec((None, key_tile * SB_HEADS, SB_HEAD_DIM),
                              lambda b, kt: (b, n_kt - 1 - kt, 0))
    return pl.pallas_call(
        functools.partial(_sb_decode_kernel, new=new, key_tile=key_tile, bk=bk),
        grid=(batch, n_kt),
        in_specs=[q_tile, new_tile, new_tile, cache_tile, cache_tile, _resident((2 * bk, bk))],
        out_specs=q_tile,
        out_shape=jax.ShapeDtypeStruct(sq.shape, BF16),
        scratch_shapes=[pltpu.VMEM((rows, D_MODEL), BF16), pltpu.VMEM((rows, D_MODEL), F32),
                        pltpu.VMEM((rows, 1), F32)],
        compiler_params=pltpu.CompilerParams(
            dimension_semantics=("arbitrary", "arbitrary"), vmem_limit_bytes=VMEM_LIMIT_BYTES),
        name="sb_decode",
    )(sq, pad_keys(sk), pad_keys(sv), cache_k, cache_v, _suffix_matrix(bk))


def _out_mlp_kernel(x_ref, r_ref, s_ref, g2_ref, wo_ref, nw_ref, wu_ref, wd_ref, y_ref, *, chains):
    n = x_ref.shape[0] // chains
    rows = [slice(c * n, (c + 1) * n) for c in range(chains)]
    mix = [(r_ref[rw, :].astype(F32) + g2_ref[rw, :].astype(F32) * s_ref[rw, :].astype(F32)).astype(BF16)
           for rw in rows]
    x1 = [x_ref[rw, :] + _dot(m, wo_ref[...]) for rw, m in zip(rows, mix)]
    h = [(_rms(v) * nw_ref[...]).astype(BF16) for v in x1]
    up = [jnp.maximum(_dot(v, wu_ref[...]), 0.0) for v in h]
    act = [(u * u).astype(BF16) for u in up]
    for rw, v, a in zip(rows, x1, act):
        y_ref[rw, :] = v + _dot(a, wd_ref[...])


def _out_mlp(x, r, s, g2, w_out_bf, norm_w, w_up_bf, w_down_bf):
    n = x.shape[0]
    chains = OUT_MLP_CHAINS if n % (OUT_MLP_CHAINS * TOKEN_TILE) == 0 else 1
    tm = chains * TOKEN_TILE
    assert n % tm == 0
    tile = pl.BlockSpec((tm, D_MODEL), lambda i: (i, 0))
    return pl.pallas_call(
        functools.partial(_out_mlp_kernel, chains=chains),
        grid=(n // tm,),
        in_specs=[tile, tile, tile, tile, _resident((D_MODEL, D_MODEL)), _resident((1, D_MODEL)),
                  _resident((D_MODEL, D_FF)), _resident((D_FF, D_MODEL))],
        out_specs=tile,
        out_shape=jax.ShapeDtypeStruct((n, D_MODEL), F32),
        compiler_params=pltpu.CompilerParams(
            dimension_semantics=("arbitrary",), vmem_limit_bytes=VMEM_LIMIT_BYTES),
        name="out_mlp",
    )(x, r, s, g2, w_out_bf, norm_w, w_up_bf, w_down_bf)


def _rope_tables(pos):
    half = RET_QK_DIM // 2
    inv_freq = ROPE_BASE ** (-jnp.arange(half, dtype=F32) / half)
    ang = pos.astype(F32)[:, None] * inv_freq[None, :]
    return jnp.cos(ang), jnp.sin(ang)


def kernel(x_prompt, x_sample, cache_sb_k, cache_sb_v, state_ret, norm_mix_w, w_in, ret_norm_w,
           sb_q_norm_w, sb_k_norm_w, w_out, norm_mlp_w, w_up, w_down):
    depth = w_in.shape[0]
    batch, frames, _ = x_prompt.shape
    dec_batch, new, _ = x_sample.shape
    past = cache_sb_k.shape[2]

    cos_p, sin_p = _rope_tables(jnp.arange(frames, dtype=jnp.int32))
    cos_s, sin_s = _rope_tables(past + jnp.arange(new, dtype=jnp.int32))
    reps = max(PROJ_CHAINS * TOKEN_TILE // new, 1)
    cos_s, sin_s = jnp.tile(cos_s, (reps, 1)), jnp.tile(sin_s, (reps, 1))

    xp = x_prompt.reshape(batch * frames, D_MODEL)
    xs = x_sample.reshape(dec_batch * new, D_MODEL)
    outs = [[] for _ in range(6)]
    for l in range(depth):
        w_in_bf = w_in[l].astype(BF16)
        w_out_bf = w_out[l].astype(BF16)
        w_up_bf = w_up[l].astype(BF16)
        w_down_bf = w_down[l].astype(BF16)
        nmix = norm_mix_w[l].reshape(1, D_MODEL)
        nmlp = norm_mlp_w[l].reshape(1, D_MODEL)
        qnw = sb_q_norm_w[l].reshape(1, SB_HEAD_DIM)
        knw = sb_k_norm_w[l].reshape(1, SB_HEAD_DIM)

        rq, rk, rv, g1, sq, sk, skb, sv, svb, g2 = _project(xp, nmix, w_in_bf, cos_p, sin_p, qnw, knw, True)
        r, state_p = _retention(rq, rk, rv, g1, ret_norm_w[l], None, batch, frames)
        o_sb = _sb_prompt(sq, skb, svb, batch, frames)
        xp = _out_mlp(xp, r, o_sb, g2, w_out_bf, nmlp, w_up_bf, w_down_bf)
        outs[0].append(sk.reshape(batch, frames, SB_HEADS, SB_HEAD_DIM))
        outs[1].append(sv.reshape(batch, frames, SB_HEADS, SB_HEAD_DIM))
        outs[2].append(state_p.astype(state_ret.dtype))

        rq, rk, rv, g1, sq, sk, skb, sv, svb, g2 = _project(xs, nmix, w_in_bf, cos_s, sin_s, qnw, knw, False)
        r, state_s = _retention(rq, rk, rv, g1, ret_norm_w[l], state_ret[l].astype(F32), dec_batch, new)
        o_sb = _sb_decode(sq, skb, svb,
                          cache_sb_k[l].reshape(dec_batch, past * SB_HEADS, SB_HEAD_DIM),
                          cache_sb_v[l].reshape(dec_batch, past * SB_HEADS, SB_HEAD_DIM), dec_batch, new)
        xs = _out_mlp(xs, r, o_sb, g2, w_out_bf, nmlp, w_up_bf, w_down_bf)
        outs[3].append(sk.reshape(dec_batch, new, SB_HEADS, SB_HEAD_DIM))
        outs[4].append(sv.reshape(dec_batch, new, SB_HEADS, SB_HEAD_DIM))
        outs[5].append(state_s.astype(state_ret.dtype))

    return (xp.reshape(batch, frames, D_MODEL), xs.reshape(dec_batch, new, D_MODEL),
            jnp.stack(outs[0]), jnp.stack(outs[1]), jnp.stack(outs[2]),
            jnp.stack(outs[3]), jnp.stack(outs[4]), jnp.stack(outs[5]))
```

```python
import functools
import math

import jax
import jax.numpy as jnp
from jax import lax
from jax.experimental import pallas as pl
from jax.experimental.pallas import tpu as pltpu

D_MODEL = 1024
CHUNK = 64
RET_HEADS = 4
RET_QK_DIM = 256
RET_V_DIM = D_MODEL // RET_HEADS
SB_HEADS = 8
SB_HEAD_DIM = D_MODEL // SB_HEADS
D_FF = 4 * D_MODEL
ROPE_BASE = 10000.0
EPS = 1e-6
LOG2_E = math.log2(math.e)
N_GROUPS = 9

F32 = jnp.float32
BF16 = jnp.bfloat16

VMEM_LIMIT_BYTES = 56 * 1024 * 1024
TOKEN_TILE = 256
PROJ_CHAINS = 2
OUT_MLP_CHAINS = 2
RET_BLOCK = 256
RET_STEP = 2048
RET_HEADS_PER_STEP = 2
RET_HEADS_PER_SHORT_STEP = 4
SB_BLOCK = 256
SCAN_SEGMENT = 8
SB_KEY_BLOCKS_PER_TRIP = 4
FINISH_LAG = 2
LOGITS_AHEAD = 2
SUBLANES = 8
SB_PROMPT_QUERY_SCALE = SB_HEAD_DIM ** -0.5 * 0.5
SB_DECODE_QUERY_SCALE = SB_HEAD_DIM ** -0.5 * LOG2_E
MASKED_LOG_BETA = -1e30
DECODE_KEY_TILE = 1024


def _nt_dot(a, b):
    return lax.dot_general(a, b, (((1,), (1,)), ((), ())), preferred_element_type=F32)


def _tn_dot(a, b):
    return lax.dot_general(a, b, (((0,), (0,)), ((), ())), preferred_element_type=F32)


def _dot(a, b):
    return jnp.dot(a, b, preferred_element_type=F32)


def _sigmoid(x):
    return 1.0 / (1.0 + jnp.exp(-x))


def _rms(x):
    return x * lax.rsqrt(jnp.mean(x * x, axis=-1, keepdims=True) + EPS)


def _resident(shape):
    return pl.BlockSpec(shape, lambda *_: (0,) * len(shape), pipeline_mode=pl.Buffered(1))


def _proj_kernel(*refs, key_major):
    if key_major:
        (x_ref, nw_ref, w_ref, cos_ref, sin_ref, qnw_ref, knw_ref, perm_ref, perm_t_ref,
         rq_ref, rk_ref, rv_ref, g1_ref, sq_ref, skf_ref, skb_ref, svf_ref, svb_ref, g2_ref) = refs
    else:
        (x_ref, nw_ref, w_ref, cos_ref, sin_ref, qnw_ref, knw_ref,
         rq_ref, rk_ref, rv_ref, g1_ref, sq_ref, skf_ref, skb_ref, svf_ref, svb_ref, g2_ref) = refs
    chains = x_ref.shape[0] // TOKEN_TILE
    blocks = [slice(c * TOKEN_TILE, (c + 1) * TOKEN_TILE) for c in range(chains)]
    h = [(_rms(x_ref[rw, :]) * nw_ref[...]).astype(BF16) for rw in blocks]

    def group(g):
        return [_dot(hc, w_ref[:, g * D_MODEL:(g + 1) * D_MODEL]) for hc in h]

    half = RET_QK_DIM // 2

    def rotary(p, out_ref, rw, scale):
        cos = cos_ref[rw, :]
        sin = sin_ref[rw, :]
        for hd in range(RET_HEADS):
            lo = hd * RET_QK_DIM
            x1 = p[:, lo:lo + half]
            x2 = p[:, lo + half:lo + 2 * half]
            o1 = x1 * cos - x2 * sin
            o2 = x1 * sin + x2 * cos
            if scale != 1.0:
                o1 = o1 * scale
                o2 = o2 * scale
            out_ref[rw, lo:lo + half] = o1.astype(out_ref.dtype)
            out_ref[rw, lo + half:lo + 2 * half] = o2.astype(out_ref.dtype)

    def head_norm(p, w, scale=1.0):
        ys = []
        for hd in range(SB_HEADS):
            y = _rms(p[:, hd * SB_HEAD_DIM:(hd + 1) * SB_HEAD_DIM]) * w
            ys.append(y if scale == 1.0 else y * scale)
        return jnp.concatenate(ys, axis=1)

    for rw, p in zip(blocks, group(0)):
        rotary(p, rq_ref, rw, 1.0)
    for rw, p in zip(blocks, group(1)):
        rotary(p, rk_ref, rw, RET_QK_DIM ** -0.5)
    for rw, p in zip(blocks, group(2)):
        rv_ref[rw, :] = p.astype(BF16)
    rg = group(3)
    query_scale = SB_PROMPT_QUERY_SCALE if key_major else SB_DECODE_QUERY_SCALE
    for rw, p in zip(blocks, group(4)):
        sq_ref[rw, :] = head_norm(p, qnw_ref[...], query_scale).astype(BF16)
    for c, (rw, p) in enumerate(zip(blocks, group(5))):
        sk = head_norm(p, knw_ref[...])
        skf_ref[rw, :] = sk
        skb_ref[rw, :] = (_dot(perm_ref[...], sk.astype(BF16)) if key_major else sk).astype(BF16)
    for c, (rw, sv) in enumerate(zip(blocks, group(6))):
        svf_ref[rw, :] = sv
        if key_major:
            svb_ref[c] = _tn_dot(sv.astype(BF16), perm_t_ref[...]).astype(BF16)
        else:
            svb_ref[rw, :] = sv.astype(BF16)
    for rw, r, a in zip(blocks, rg, group(7)):
        g1_ref[rw, :] = (r * _sigmoid(r) * _sigmoid(a)).astype(BF16)
    for rw, p in zip(blocks, group(8)):
        g2_ref[rw, :] = _sigmoid(p).astype(BF16)


def _key_order(n):
    row = jnp.arange(n)
    key = (row % SUBLANES) * (n // SUBLANES) + row // SUBLANES
    return (key[:, None] == jnp.arange(n)[None, :]).astype(BF16)


def _project(x, norm_w, w_in_bf, cos_tab, sin_tab, qnw, knw, key_major):
    n = x.shape[0]
    tm = PROJ_CHAINS * TOKEN_TILE
    assert n % tm == 0 and cos_tab.shape[0] % tm == 0
    pos_blocks = cos_tab.shape[0] // tm
    tile = pl.BlockSpec((tm, D_MODEL), lambda i: (i, 0))
    pos_tile = pl.BlockSpec((tm, RET_QK_DIM // 2), lambda i: (i % pos_blocks, 0))
    bf = jax.ShapeDtypeStruct((n, D_MODEL), BF16)
    f32 = jax.ShapeDtypeStruct((n, D_MODEL), F32)
    in_specs = [tile, _resident((1, D_MODEL)), _resident((D_MODEL, N_GROUPS * D_MODEL)),
                pos_tile, pos_tile, _resident((1, SB_HEAD_DIM)), _resident((1, SB_HEAD_DIM))]
    args = [x, norm_w, w_in_bf, cos_tab, sin_tab, qnw, knw]
    out_specs = [tile] * 10
    out_shape = [bf, bf, bf, bf, bf, f32, bf, f32, bf, bf]
    if key_major:
        blk = TOKEN_TILE
        assert blk == SB_BLOCK
        perm = _key_order(blk)
        in_specs += [_resident((blk, blk)), _resident((blk, blk))]
        args += [perm, (perm.T.astype(F32) * _value_scale(blk)[None, :]).astype(BF16)]
        out_specs[8] = pl.BlockSpec((PROJ_CHAINS, D_MODEL, blk), lambda i: (i, 0, 0))
        out_shape[8] = jax.ShapeDtypeStruct((n // blk, D_MODEL, blk), BF16)
    return pl.pallas_call(
        functools.partial(_proj_kernel, key_major=key_major),
        grid=(n // tm,),
        in_specs=in_specs,
        out_specs=out_specs,
        out_shape=out_shape,
        compiler_params=pltpu.CompilerParams(
            dimension_semantics=("arbitrary",), vmem_limit_bytes=VMEM_LIMIT_BYTES),
        name="proj",
    )(*args)


def _retention_kernel(*refs, n_sub, block, heads, has_state):
    if has_state:
        gl_ref, q_ref, k_ref, v_ref, g1_ref, dec_ref, qd_ref, kd_ref, rnw_ref, s0_ref = refs[:10]
        rest = refs[10:]
    else:
        gl_ref, q_ref, k_ref, v_ref, g1_ref, dec_ref, qd_ref, kd_ref, rnw_ref = refs[:9]
        s0_ref = None
        rest = refs[9:]
    r_ref, sout_ref, s_ref = rest
    first_head = pl.program_id(1) * heads
    t = pl.program_id(2)

    @pl.when(t == 0)
    def _():
        if has_state:
            s_ref[...] = s0_ref[...]
        else:
            s_ref[...] = jnp.zeros_like(s_ref)

    hs = range(heads)
    cols = [slice(j * RET_QK_DIM, (j + 1) * RET_QK_DIM) for j in hs]
    for sb in range(n_sub):
        rows = slice(sb * block, (sb + 1) * block)
        q = [q_ref[rows, c] for c in cols]
        k = [k_ref[rows, c] for c in cols]
        v = [v_ref[rows, c] for c in cols]
        s_prev = [s_ref[j] for j in hs]
        scores = [(_nt_dot(q[j], k[j]) * dec_ref[j]).astype(BF16) for j in hs]
        cross = [_dot(q[j], s_prev[j].astype(BF16)) * qd_ref[j] for j in hs]
        k_dec = [(k[j].astype(F32) * kd_ref[j]).astype(BF16) for j in hs]
        for j in hs:
            s_ref[j] = gl_ref[first_head + j] * s_prev[j] + _tn_dot(k_dec[j], v[j])
        for j in hs:
            r = _rms(_dot(scores[j], v[j]) + cross[j]) * rnw_ref[j]
            r_ref[rows, cols[j]] = (r * g1_ref[rows, cols[j]].astype(F32)).astype(BF16)

    @pl.when(t == pl.num_programs(2) - 1)
    def _():
        sout_ref[...] = s_ref[...]


def _retention_tables(block):
    log_gamma = jnp.log1p(-jnp.exp2(-5.0 - jnp.arange(RET_HEADS, dtype=F32)))
    idx = jnp.arange(block, dtype=F32)
    dist = jnp.abs(idx[:, None] - idx[None, :])
    chunk = jnp.arange(block, dtype=jnp.int32) // CHUNK
    visible = chunk[None, :] <= chunk[:, None]
    dec = jnp.where(visible[None], jnp.exp(log_gamma[:, None, None] * dist[None]), 0.0)
    qd = jnp.exp(log_gamma[:, None] * (idx[None, :] + 1.0))
    kd = jnp.exp(log_gamma[:, None] * (block - 1.0 - idx[None, :]))
    widen = lambda a: jnp.broadcast_to(a[:, :, None], (RET_HEADS, block, RET_QK_DIM))
    gl = jnp.exp(log_gamma * block)
    return gl, dec, widen(qd), widen(kd)


def _retention(rq, rk, rv, g1, ret_norm_w, state0, batch, frames):
    block = min(RET_BLOCK, frames)
    step = min(RET_STEP, frames)
    assert frames % step == 0 and step % block == 0
    assert block % CHUNK == 0 or block < CHUNK
    n_t = frames // step
    gl, dec, qd, kd = _retention_tables(block)
    has_state = state0 is not None
    heads = RET_HEADS_PER_SHORT_STEP if step == block else RET_HEADS_PER_STEP
    assert RET_HEADS % heads == 0
    tile = pl.BlockSpec((step, heads * RET_QK_DIM), lambda b, h, t: (b * n_t + t, h))
    per_head = lambda rows, cols: pl.BlockSpec((heads, rows, cols), lambda b, h, t: (h, 0, 0))
    state_spec = pl.BlockSpec((None, heads, RET_QK_DIM, RET_V_DIM), lambda b, h, t: (b, h, 0, 0))
    in_specs = [pl.BlockSpec(memory_space=pltpu.SMEM), tile, tile, tile, tile,
                per_head(block, block), per_head(block, RET_QK_DIM), per_head(block, RET_QK_DIM),
                per_head(1, RET_V_DIM)]
    args = [gl, rq, rk, rv, g1, dec, qd, kd, ret_norm_w.reshape(RET_HEADS, 1, RET_V_DIM)]
    if has_state:
        in_specs.append(state_spec)
        args.append(state0)
    return pl.pallas_call(
        functools.partial(_retention_kernel, n_sub=step // block, block=block, heads=heads,
                          has_state=has_state),
        grid=(batch, RET_HEADS // heads, n_t),
        in_specs=in_specs,
        out_specs=[tile, state_spec],
        out_shape=[jax.ShapeDtypeStruct(rq.shape, BF16),
                   jax.ShapeDtypeStruct((batch, RET_HEADS, RET_QK_DIM, RET_V_DIM), F32)],
        scratch_shapes=[pltpu.VMEM((heads, RET_QK_DIM, RET_V_DIM), F32)],
        compiler_params=pltpu.CompilerParams(
            dimension_semantics=("arbitrary", "arbitrary", "arbitrary"),
            vmem_limit_bytes=VMEM_LIMIT_BYTES),
        name="retention",
    )(*args)


def _suffix_matrix(n):
    idx = jnp.arange(n)
    m = (idx[:, None] > idx[None, :]).astype(BF16)
    return jnp.concatenate([m, m], axis=0)


def _sb_gates(z, mask):
    neg_abs = pltpu.bitcast(pltpu.bitcast(z, jnp.uint32) | jnp.uint32(0x80000000), F32)
    soft = jnp.log2(1.0 + jnp.exp2(neg_abs))
    log_beta = jnp.minimum(z, 0.0) - soft
    u = log_beta - z
    if mask is not None:
        u = jnp.where(mask, u, 0.0)
        log_beta = jnp.where(mask, log_beta, MASKED_LOG_BETA)
    u_hi = u.astype(BF16)
    u_lo = (u - u_hi.astype(F32)).astype(BF16)
    return log_beta, jnp.concatenate([u_hi, u_lo], axis=1), jnp.sum(u, axis=-1, keepdims=True)


def _sb_weights(log_beta, after, carry):
    return jnp.exp2(log_beta + carry + after).astype(BF16)


def _sb_key_run(q, ks, vs, tri2, carry, mask):
    n = len(ks)
    z = [_nt_dot(q, ks[i]) for i in range(n)]
    gates = [_sb_gates(z[i], mask) for i in range(n)]
    after = [_dot(gates[i][1], tri2) for i in range(n)]
    out = None
    for i in range(n):
        o = _dot(_sb_weights(gates[i][0], after[i], carry), vs[i])
        out = o if out is None else out + o
        carry = carry + gates[i][2]
    return out, carry


def _sb_doubled_gates(half_z, visible):
    t = jnp.tanh(half_z)
    if visible is not None:
        t = jnp.where(visible, t, -1.0)
    return 1.0 + t, 1.0 - t


def _sb_run_scan(half_z, visible):
    groups = half_z.shape[0] // SUBLANES
    pieces = [None] * groups
    totals = []
    for first in range(0, groups, SCAN_SEGMENT):
        run = None
        for r in reversed(range(first, first + SCAN_SEGMENT)):
            rows = slice(r * SUBLANES, (r + 1) * SUBLANES)
            beta2, rest2 = _sb_doubled_gates(half_z[rows], None if visible is None else visible[rows])
            pieces[r] = beta2 if run is None else beta2 * run
            run = rest2 if run is None else run * rest2
        totals.append(run * 2.0 ** -SCAN_SEGMENT)
    return pieces, jnp.concatenate(totals, axis=0)


def _value_scale(n):
    place = (jnp.arange(n) // SUBLANES) % SCAN_SEGMENT
    return jnp.exp2((place - SCAN_SEGMENT).astype(F32))


def _suffix_over_sublanes(t):
    s = lax.broadcasted_iota(jnp.int32, t.shape, 0)
    y = t
    for k in (1, 2, 4):
        below = pltpu.roll(y, SUBLANES - k, axis=0)
        y = y * jnp.where(s + k < SUBLANES, below, 1.0)
    exclusive = jnp.where(s + 1 < SUBLANES, pltpu.roll(y, SUBLANES - 1, axis=0), 1.0)
    return exclusive, jnp.broadcast_to(y[0:1, :], t.shape)


def _sb_finish(pieces, totals, v_t, carry):
    segments = totals.shape[0] // SUBLANES
    seg_totals = [totals[j * SUBLANES:(j + 1) * SUBLANES] for j in range(segments)]
    later_segments = [None] * segments
    run_total = None
    for j in reversed(range(segments)):
        later_segments[j] = run_total
        run_total = seg_totals[j] if run_total is None else run_total * seg_totals[j]
    later_runs, total = _suffix_over_sublanes(run_total)
    base = later_runs * carry
    slab = 2 * SUBLANES
    slabs_per_segment = pieces.shape[0] // slab // segments
    w = []
    for j in range(segments):
        factor = base if later_segments[j] is None else base * later_segments[j]
        factor = jnp.concatenate([factor, factor], axis=0).astype(BF16)
        for g in range(j * slabs_per_segment, (j + 1) * slabs_per_segment):
            w.append(pieces[g * slab:(g + 1) * slab] * factor)
    return _dot(v_t, jnp.concatenate(w, axis=0)), carry * total


def _sb_prompt_kernel(q_ref, k_ref, vt_ref, o_ref, acc_ref, carry_ref, z_ref, pieces_ref, totals_ref,
                      *, blk):
    i = pl.program_id(1)
    heads = SB_HEADS
    cols = [slice(hd * SB_HEAD_DIM, (hd + 1) * SB_HEAD_DIM) for hd in range(heads)]
    parked = tuple(range(heads - FINISH_LAG, heads))

    def logits(block, hd):
        start = pl.multiple_of(block * blk, blk)
        return _nt_dot(k_ref[pl.ds(start, blk), cols[hd]], q_ref[:, cols[hd]])

    def finish(pieces, totals, block, hd):
        out, carry = _sb_finish(pieces, totals, vt_ref[block, cols[hd], :], carry_ref[hd])
        acc_ref[hd] += out
        carry_ref[hd] = carry

    def key_block(block, visible, have_parked):
        z_queue = [z_ref[ahead] for ahead in range(LOGITS_AHEAD)]
        scans = {}
        for hd in range(heads):
            ahead = [hd + LOGITS_AHEAD] if hd + LOGITS_AHEAD < heads else []
            if hd == heads - LOGITS_AHEAD - 1:
                ahead += range(heads, heads + LOGITS_AHEAD)
            for nxt in ahead:
                z_queue.append(logits(block, nxt) if nxt < heads
                               else logits(jnp.maximum(block - 1, 0), nxt - heads))
            if hd < len(parked):
                if have_parked:
                    finish(pieces_ref[hd], totals_ref[hd], block + 1, parked[hd])
            else:
                finish(*scans.pop(hd - len(parked)), block, hd - len(parked))
            pieces, totals = _sb_run_scan(z_queue.pop(0), visible)
            scans[hd] = (jnp.concatenate(pieces, axis=0).astype(BF16), totals)
        for slot, hd in enumerate(parked):
            pieces_ref[slot], totals_ref[slot] = scans[hd]
        for ahead in range(LOGITS_AHEAD):
            z_ref[ahead] = z_queue[ahead]

    acc_ref[...] = jnp.zeros_like(acc_ref)
    carry_ref[...] = jnp.ones_like(carry_ref)
    for ahead in range(LOGITS_AHEAD):
        z_ref[ahead] = logits(i, ahead)
    row = lax.broadcasted_iota(jnp.int32, (blk, blk), 0)
    key = (row & (SUBLANES - 1)) * (blk // SUBLANES) + (row >> (SUBLANES.bit_length() - 1))
    key_block(i, key < lax.broadcasted_iota(jnp.int32, (blk, blk), 1), False)

    for p in [1 << b for b in range(SB_KEY_BLOCKS_PER_TRIP.bit_length() - 1)]:
        @pl.when((i & p) != 0)
        def _(p=p):
            top = i - 1 - (i & (p - 1))
            for d in range(p):
                key_block(top - d, None, True)

    def trip(t, _):
        top = i - 1 - (i & (SB_KEY_BLOCKS_PER_TRIP - 1)) - SB_KEY_BLOCKS_PER_TRIP * t
        for d in range(SB_KEY_BLOCKS_PER_TRIP):
            key_block(top - d, None, True)
        return 0

    lax.fori_loop(0, i // SB_KEY_BLOCKS_PER_TRIP, trip, 0)
    for slot, hd in enumerate(parked):
        finish(pieces_ref[slot], totals_ref[slot], 0, hd)
    for hd in range(heads):
        o_ref[:, cols[hd]] = acc_ref[hd].T.astype(o_ref.dtype)


def _sb_prompt(sq, sk_km, sv_t, batch, frames):
    blk = SB_BLOCK
    assert frames % blk == 0
    n_q = frames // blk
    q_tile = pl.BlockSpec((blk, D_MODEL), lambda b, i: (b * n_q + i, 0))
    k_all = pl.BlockSpec((frames, D_MODEL), lambda b, i: (b, 0))
    vt_all = pl.BlockSpec((None, n_q, D_MODEL, blk), lambda b, i: (b, 0, 0, 0))
    return pl.pallas_call(
        functools.partial(_sb_prompt_kernel, blk=blk),
        grid=(batch, n_q),
        in_specs=[q_tile, k_all, vt_all],
        out_specs=q_tile,
        out_shape=jax.ShapeDtypeStruct(sq.shape, BF16),
        scratch_shapes=[pltpu.VMEM((SB_HEADS, SB_HEAD_DIM, blk), F32),
                        pltpu.VMEM((SB_HEADS, SUBLANES, blk), F32),
                        pltpu.VMEM((LOGITS_AHEAD, blk, blk), F32),
                        pltpu.VMEM((FINISH_LAG, blk, blk), BF16),
                        pltpu.VMEM((FINISH_LAG, blk // SCAN_SEGMENT, blk), F32)],
        compiler_params=pltpu.CompilerParams(
            dimension_semantics=("arbitrary", "arbitrary"), vmem_limit_bytes=VMEM_LIMIT_BYTES),
        name="sb_prompt",
    )(sq, sk_km, sv_t.reshape(batch, n_q, D_MODEL, blk))


def _sb_decode_kernel(q_ref, kn_ref, vn_ref, kc_ref, vc_ref, tri_ref, o_ref,
                      qbd_ref, acc_ref, carry_ref, *, new, key_tile, bk):
    kt = pl.program_id(1)
    rows = SB_HEADS * new
    tri2 = tri_ref[...]

    @pl.when(kt == 0)
    def _():
        q = q_ref[...]
        col = lax.broadcasted_iota(jnp.int32, (new, D_MODEL), 1)
        for hd in range(SB_HEADS):
            keep = (col >= hd * SB_HEAD_DIM) & (col < (hd + 1) * SB_HEAD_DIM)
            qbd_ref[hd * new:(hd + 1) * new, :] = jnp.where(keep, q, jnp.zeros_like(q))
        l_q = jnp.concatenate([lax.broadcasted_iota(jnp.int32, (new, new), 0)] * SB_HEADS, axis=0)
        l_k = lax.broadcasted_iota(jnp.int32, (rows, new), 1)
        tri2_new = jnp.concatenate([tri2[:new, :new], tri2[bk:bk + new, :new]], axis=0)
        out, carry = _sb_key_run(qbd_ref[...], [kn_ref[...]], [vn_ref[...]], tri2_new,
                                 jnp.zeros((rows, 1), F32), l_k < l_q)
        acc_ref[...] = out
        carry_ref[...] = carry

    def head_major(ref, sb):
        base = sb * bk * SB_HEADS
        return jnp.concatenate(
            [ref[pl.ds(base + hd, bk, stride=SB_HEADS), :].astype(BF16) for hd in range(SB_HEADS)], axis=1)

    order = list(reversed(range(key_tile // bk)))
    out, carry = _sb_key_run(qbd_ref[...], [head_major(kc_ref, sb) for sb in order],
                             [head_major(vc_ref, sb) for sb in order], tri2, carry_ref[...], None)
    acc_ref[...] += out
    carry_ref[...] = carry

    @pl.when(kt == pl.num_programs(1) - 1)
    def _():
        for hd in range(SB_HEADS):
            cols = slice(hd * SB_HEAD_DIM, (hd + 1) * SB_HEAD_DIM)
            o_ref[:, cols] = acc_ref[hd * new:(hd + 1) * new, cols].astype(o_ref.dtype)


def _sb_decode(sq, sk, sv, cache_k, cache_v, batch, new):
    past = cache_k.shape[1] // SB_HEADS
    key_tile = min(DECODE_KEY_TILE, past)
    bk = min(SB_BLOCK, key_tile)
    assert past % key_tile == 0 and key_tile % bk == 0 and new <= bk
    n_kt = past // key_tile
    rows = SB_HEADS * new
    q_tile = pl.BlockSpec((new, D_MODEL), lambda b, kt: (b, 0))
    cache_tile = pl.BlockSpec((None, key_tile * SB_HEADS, SB_HEAD_DIM),
                              lambda b, kt: (b, n_kt - 1 - kt, 0))
    return pl.pallas_call(
        functools.partial(_sb_decode_kernel, new=new, key_tile=key_tile, bk=bk),
        grid=(batch, n_kt),
        in_specs=[q_tile, q_tile, q_tile, cache_tile, cache_tile, _resident((2 * bk, bk))],
        out_specs=q_tile,
        out_shape=jax.ShapeDtypeStruct(sq.shape, BF16),
        scratch_shapes=[pltpu.VMEM((rows, D_MODEL), BF16), pltpu.VMEM((rows, D_MODEL), F32),
                        pltpu.VMEM((rows, 1), F32)],
        compiler_params=pltpu.CompilerParams(
            dimension_semantics=("arbitrary", "arbitrary"), vmem_limit_bytes=VMEM_LIMIT_BYTES),
        name="sb_decode",
    )(sq, sk, sv, cache_k, cache_v, _suffix_matrix(bk))


def _out_mlp_kernel(x_ref, r_ref, s_ref, g2_ref, wo_ref, nw_ref, wu_ref, wd_ref, y_ref, *, chains):
    n = x_ref.shape[0] // chains
    rows = [slice(c * n, (c + 1) * n) for c in range(chains)]
    mix = [(r_ref[rw, :].astype(F32) + g2_ref[rw, :].astype(F32) * s_ref[rw, :].astype(F32)).astype(BF16)
           for rw in rows]
    x1 = [x_ref[rw, :] + _dot(m, wo_ref[...]) for rw, m in zip(rows, mix)]
    h = [(_rms(v) * nw_ref[...]).astype(BF16) for v in x1]
    up = [jnp.maximum(_dot(v, wu_ref[...]), 0.0) for v in h]
    act = [(u * u).astype(BF16) for u in up]
    for rw, v, a in zip(rows, x1, act):
        y_ref[rw, :] = v + _dot(a, wd_ref[...])


def _out_mlp(x, r, s, g2, w_out_bf, norm_w, w_up_bf, w_down_bf):
    n = x.shape[0]
    chains = OUT_MLP_CHAINS if n % (OUT_MLP_CHAINS * TOKEN_TILE) == 0 else 1
    tm = chains * TOKEN_TILE
    assert n % tm == 0
    tile = pl.BlockSpec((tm, D_MODEL), lambda i: (i, 0))
    return pl.pallas_call(
        functools.partial(_out_mlp_kernel, chains=chains),
        grid=(n // tm,),
        in_specs=[tile, tile, tile, tile, _resident((D_MODEL, D_MODEL)), _resident((1, D_MODEL)),
                  _resident((D_MODEL, D_FF)), _resident((D_FF, D_MODEL))],
        out_specs=tile,
        out_shape=jax.ShapeDtypeStruct((n, D_MODEL), F32),
        compiler_params=pltpu.CompilerParams(
            dimension_semantics=("arbitrary",), vmem_limit_bytes=VMEM_LIMIT_BYTES),
        name="out_mlp",
    )(x, r, s, g2, w_out_bf, norm_w, w_up_bf, w_down_bf)


def _rope_tables(pos):
    half = RET_QK_DIM // 2
    inv_freq = ROPE_BASE ** (-jnp.arange(half, dtype=F32) / half)
    ang = pos.astype(F32)[:, None] * inv_freq[None, :]
    return jnp.cos(ang), jnp.sin(ang)


def kernel(x_prompt, x_sample, cache_sb_k, cache_sb_v, state_ret, norm_mix_w, w_in, ret_norm_w,
           sb_q_norm_w, sb_k_norm_w, w_out, norm_mlp_w, w_up, w_down):
    depth = w_in.shape[0]
    batch, frames, _ = x_prompt.shape
    dec_batch, new, _ = x_sample.shape
    past = cache_sb_k.shape[2]

    cos_p, sin_p = _rope_tables(jnp.arange(frames, dtype=jnp.int32))
    cos_s, sin_s = _rope_tables(past + jnp.arange(new, dtype=jnp.int32))
    reps = max(PROJ_CHAINS * TOKEN_TILE // new, 1)
    cos_s, sin_s = jnp.tile(cos_s, (reps, 1)), jnp.tile(sin_s, (reps, 1))

    xp = x_prompt.reshape(batch * frames, D_MODEL)
    xs = x_sample.reshape(dec_batch * new, D_MODEL)
    outs = [[] for _ in range(6)]
    for l in range(depth):
        w_in_bf = w_in[l].astype(BF16)
        w_out_bf = w_out[l].astype(BF16)
        w_up_bf = w_up[l].astype(BF16)
        w_down_bf = w_down[l].astype(BF16)
        nmix = norm_mix_w[l].reshape(1, D_MODEL)
        nmlp = norm_mlp_w[l].reshape(1, D_MODEL)
        qnw = sb_q_norm_w[l].reshape(1, SB_HEAD_DIM)
        knw = sb_k_norm_w[l].reshape(1, SB_HEAD_DIM)

        rq, rk, rv, g1, sq, sk, skb, sv, svb, g2 = _project(xp, nmix, w_in_bf, cos_p, sin_p, qnw, knw, True)
        r, state_p = _retention(rq, rk, rv, g1, ret_norm_w[l], None, batch, frames)
        o_sb = _sb_prompt(sq, skb, svb, batch, frames)
        xp = _out_mlp(xp, r, o_sb, g2, w_out_bf, nmlp, w_up_bf, w_down_bf)
        outs[0].append(sk.reshape(batch, frames, SB_HEADS, SB_HEAD_DIM))
        outs[1].append(sv.reshape(batch, frames, SB_HEADS, SB_HEAD_DIM))
        outs[2].append(state_p.astype(state_ret.dtype))

        rq, rk, rv, g1, sq, sk, skb, sv, svb, g2 = _project(xs, nmix, w_in_bf, cos_s, sin_s, qnw, knw, False)
        r, state_s = _retention(rq, rk, rv, g1, ret_norm_w[l], state_ret[l].astype(F32), dec_batch, new)
        o_sb = _sb_decode(sq, skb, svb,
                          cache_sb_k[l].reshape(dec_batch, past * SB_HEADS, SB_HEAD_DIM),
                          cache_sb_v[l].reshape(dec_batch, past * SB_HEADS, SB_HEAD_DIM), dec_batch, new)
        xs = _out_mlp(xs, r, o_sb, g2, w_out_bf, nmlp, w_up_bf, w_down_bf)
        outs[3].append(sk.reshape(dec_batch, new, SB_HEADS, SB_HEAD_DIM))
        outs[4].append(sv.reshape(dec_batch, new, SB_HEADS, SB_HEAD_DIM))
        outs[5].append(state_s.astype(state_ret.dtype))

    return (xp.reshape(batch, frames, D_MODEL), xs.reshape(dec_batch, new, D_MODEL),
            jnp.stack(outs[0]), jnp.stack(outs[1]), jnp.stack(outs[2]),
            jnp.stack(outs[3]), jnp.stack(outs[4]), jnp.stack(outs[5]))
```

```python
import functools
import math

import jax
import jax.numpy as jnp
from jax import lax
from jax.experimental import pallas as pl
from jax.experimental.pallas import tpu as pltpu

D_MODEL = 1024
CHUNK = 64
RET_HEADS = 4
RET_QK_DIM = 256
RET_V_DIM = D_MODEL // RET_HEADS
SB_HEADS = 8
SB_HEAD_DIM = D_MODEL // SB_HEADS
D_FF = 4 * D_MODEL
ROPE_BASE = 10000.0
EPS = 1e-6
LOG2_E = math.log2(math.e)
N_GROUPS = 9

F32 = jnp.float32
BF16 = jnp.bfloat16

VMEM_LIMIT_BYTES = 56 * 1024 * 1024
TOKEN_TILE = 256
PROJ_CHAINS = 2
OUT_MLP_CHAINS = 2
RET_BLOCK = 256
RET_STEP = 4096
RET_HEADS_PER_STEP = 2
RET_HEADS_PER_SHORT_STEP = 4
SB_BLOCK = 256
SCAN_SEGMENT = 8
SB_KEY_BLOCKS_PER_TRIP = 4
FINISH_LAG = 2
LOGITS_AHEAD = 2
SUBLANES = 8
SB_PROMPT_QUERY_SCALE = SB_HEAD_DIM ** -0.5 * 0.5
SB_DECODE_QUERY_SCALE = SB_HEAD_DIM ** -0.5 * LOG2_E
MASKED_LOG_BETA = -1e30
DECODE_KEY_TILE = 2048


def _nt_dot(a, b):
    return lax.dot_general(a, b, (((1,), (1,)), ((), ())), preferred_element_type=F32)


def _tn_dot(a, b):
    return lax.dot_general(a, b, (((0,), (0,)), ((), ())), preferred_element_type=F32)


def _dot(a, b):
    return jnp.dot(a, b, preferred_element_type=F32)


def _sigmoid(x):
    return 1.0 / (1.0 + jnp.exp(-x))


def _rms(x):
    return x * lax.rsqrt(jnp.mean(x * x, axis=-1, keepdims=True) + EPS)


def _resident(shape):
    return pl.BlockSpec(shape, lambda *_: (0,) * len(shape), pipeline_mode=pl.Buffered(1))


def _proj_kernel(*refs, key_major):
    if key_major:
        (x_ref, nw_ref, w_ref, cos_ref, sin_ref, qnw_ref, knw_ref, perm_ref, perm_t_ref,
         rq_ref, rk_ref, rv_ref, g1_ref, sq_ref, skf_ref, skb_ref, svf_ref, svb_ref, g2_ref) = refs
    else:
        (x_ref, nw_ref, w_ref, cos_ref, sin_ref, qnw_ref, knw_ref,
         rq_ref, rk_ref, rv_ref, g1_ref, sq_ref, skf_ref, skb_ref, svf_ref, svb_ref, g2_ref) = refs
    chains = x_ref.shape[0] // TOKEN_TILE
    blocks = [slice(c * TOKEN_TILE, (c + 1) * TOKEN_TILE) for c in range(chains)]
    h = [(_rms(x_ref[rw, :]) * nw_ref[...]).astype(BF16) for rw in blocks]

    def group(g):
        return [_dot(hc, w_ref[:, g * D_MODEL:(g + 1) * D_MODEL]) for hc in h]

    half = RET_QK_DIM // 2

    def rotary(p, out_ref, rw, scale):
        cos = cos_ref[rw, :]
        sin = sin_ref[rw, :]
        for hd in range(RET_HEADS):
            lo = hd * RET_QK_DIM
            x1 = p[:, lo:lo + half]
            x2 = p[:, lo + half:lo + 2 * half]
            o1 = x1 * cos - x2 * sin
            o2 = x1 * sin + x2 * cos
            if scale != 1.0:
                o1 = o1 * scale
                o2 = o2 * scale
            out_ref[rw, lo:lo + half] = o1.astype(out_ref.dtype)
            out_ref[rw, lo + half:lo + 2 * half] = o2.astype(out_ref.dtype)

    def head_norm(p, w, scale=1.0):
        ys = []
        for hd in range(SB_HEADS):
            y = _rms(p[:, hd * SB_HEAD_DIM:(hd + 1) * SB_HEAD_DIM]) * w
            ys.append(y if scale == 1.0 else y * scale)
        return jnp.concatenate(ys, axis=1)

    for rw, p in zip(blocks, group(0)):
        rotary(p, rq_ref, rw, 1.0)
    for rw, p in zip(blocks, group(1)):
        rotary(p, rk_ref, rw, RET_QK_DIM ** -0.5)
    for rw, p in zip(blocks, group(2)):
        rv_ref[rw, :] = p.astype(BF16)
    rg = group(3)
    query_scale = SB_PROMPT_QUERY_SCALE if key_major else SB_DECODE_QUERY_SCALE
    for rw, p in zip(blocks, group(4)):
        sq_ref[rw, :] = head_norm(p, qnw_ref[...], query_scale).astype(BF16)
    for c, (rw, p) in enumerate(zip(blocks, group(5))):
        sk = head_norm(p, knw_ref[...])
        skf_ref[rw, :] = sk
        skb_ref[rw, :] = (_dot(perm_ref[...], sk.astype(BF16)) if key_major else sk).astype(BF16)
    for c, (rw, sv) in enumerate(zip(blocks, group(6))):
        svf_ref[rw, :] = sv
        if key_major:
            svb_ref[c] = _tn_dot(sv.astype(BF16), perm_t_ref[...]).astype(BF16)
        else:
            svb_ref[rw, :] = sv.astype(BF16)
    for rw, r, a in zip(blocks, rg, group(7)):
        g1_ref[rw, :] = (r * _sigmoid(r) * _sigmoid(a)).astype(BF16)
    for rw, p in zip(blocks, group(8)):
        g2_ref[rw, :] = _sigmoid(p).astype(BF16)


def _key_order(n):
    row = jnp.arange(n)
    key = (row % SUBLANES) * (n // SUBLANES) + row // SUBLANES
    return (key[:, None] == jnp.arange(n)[None, :]).astype(BF16)


def _project(x, norm_w, w_in_bf, cos_tab, sin_tab, qnw, knw, key_major):
    n = x.shape[0]
    tm = PROJ_CHAINS * TOKEN_TILE
    assert n % tm == 0 and cos_tab.shape[0] % tm == 0
    pos_blocks = cos_tab.shape[0] // tm
    tile = pl.BlockSpec((tm, D_MODEL), lambda i: (i, 0))
    pos_tile = pl.BlockSpec((tm, RET_QK_DIM // 2), lambda i: (i % pos_blocks, 0))
    bf = jax.ShapeDtypeStruct((n, D_MODEL), BF16)
    f32 = jax.ShapeDtypeStruct((n, D_MODEL), F32)
    in_specs = [tile, _resident((1, D_MODEL)), _resident((D_MODEL, N_GROUPS * D_MODEL)),
                pos_tile, pos_tile, _resident((1, SB_HEAD_DIM)), _resident((1, SB_HEAD_DIM))]
    args = [x, norm_w, w_in_bf, cos_tab, sin_tab, qnw, knw]
    out_specs = [tile] * 10
    out_shape = [bf, bf, bf, bf, bf, f32, bf, f32, bf, bf]
    if key_major:
        blk = TOKEN_TILE
        assert blk == SB_BLOCK
        perm = _key_order(blk)
        in_specs += [_resident((blk, blk)), _resident((blk, blk))]
        args += [perm, (perm.T.astype(F32) * _value_scale(blk)[None, :]).astype(BF16)]
        out_specs[8] = pl.BlockSpec((PROJ_CHAINS, D_MODEL, blk), lambda i: (i, 0, 0))
        out_shape[8] = jax.ShapeDtypeStruct((n // blk, D_MODEL, blk), BF16)
    return pl.pallas_call(
        functools.partial(_proj_kernel, key_major=key_major),
        grid=(n // tm,),
        in_specs=in_specs,
        out_specs=out_specs,
        out_shape=out_shape,
        compiler_params=pltpu.CompilerParams(
            dimension_semantics=("arbitrary",), vmem_limit_bytes=VMEM_LIMIT_BYTES),
        name="proj",
    )(*args)


def _retention_kernel(*refs, n_sub, block, heads, has_state):
    if has_state:
        gl_ref, q_ref, k_ref, v_ref, g1_ref, dec_ref, qd_ref, kd_ref, rnw_ref, s0_ref = refs[:10]
        rest = refs[10:]
    else:
        gl_ref, q_ref, k_ref, v_ref, g1_ref, dec_ref, qd_ref, kd_ref, rnw_ref = refs[:9]
        s0_ref = None
        rest = refs[9:]
    r_ref, sout_ref, s_ref = rest
    first_head = pl.program_id(1) * heads
    t = pl.program_id(2)

    @pl.when(t == 0)
    def _():
        if has_state:
            s_ref[...] = s0_ref[...]
        else:
            s_ref[...] = jnp.zeros_like(s_ref)

    hs = range(heads)
    cols = [slice(j * RET_QK_DIM, (j + 1) * RET_QK_DIM) for j in hs]
    for sb in range(n_sub):
        rows = slice(sb * block, (sb + 1) * block)
        q = [q_ref[rows, c] for c in cols]
        k = [k_ref[rows, c] for c in cols]
        v = [v_ref[rows, c] for c in cols]
        s_prev = [s_ref[j] for j in hs]
        scores = [(_nt_dot(q[j], k[j]) * dec_ref[j]).astype(BF16) for j in hs]
        cross = [_dot(q[j], s_prev[j].astype(BF16)) * qd_ref[j] for j in hs]
        k_dec = [(k[j].astype(F32) * kd_ref[j]).astype(BF16) for j in hs]
        for j in hs:
            s_ref[j] = gl_ref[first_head + j] * s_prev[j] + _tn_dot(k_dec[j], v[j])
        for j in hs:
            r = _rms(_dot(scores[j], v[j]) + cross[j]) * rnw_ref[j]
            r_ref[rows, cols[j]] = (r * g1_ref[rows, cols[j]].astype(F32)).astype(BF16)

    @pl.when(t == pl.num_programs(2) - 1)
    def _():
        sout_ref[...] = s_ref[...]


def _retention_tables(block):
    log_gamma = jnp.log1p(-jnp.exp2(-5.0 - jnp.arange(RET_HEADS, dtype=F32)))
    idx = jnp.arange(block, dtype=F32)
    dist = jnp.abs(idx[:, None] - idx[None, :])
    chunk = jnp.arange(block, dtype=jnp.int32) // CHUNK
    visible = chunk[None, :] <= chunk[:, None]
    dec = jnp.where(visible[None], jnp.exp(log_gamma[:, None, None] * dist[None]), 0.0)
    qd = jnp.exp(log_gamma[:, None] * (idx[None, :] + 1.0))
    kd = jnp.exp(log_gamma[:, None] * (block - 1.0 - idx[None, :]))
    widen = lambda a: jnp.broadcast_to(a[:, :, None], (RET_HEADS, block, RET_QK_DIM))
    gl = jnp.exp(log_gamma * block)
    return gl, dec, widen(qd), widen(kd)


def _retention(rq, rk, rv, g1, ret_norm_w, state0, batch, frames):
    block = min(RET_BLOCK, frames)
    step = min(RET_STEP, frames)
    assert frames % step == 0 and step % block == 0
    assert block % CHUNK == 0 or block < CHUNK
    n_t = frames // step
    gl, dec, qd, kd = _retention_tables(block)
    has_state = state0 is not None
    heads = RET_HEADS_PER_SHORT_STEP if step == block else RET_HEADS_PER_STEP
    assert RET_HEADS % heads == 0
    tile = pl.BlockSpec((step, heads * RET_QK_DIM), lambda b, h, t: (b * n_t + t, h))
    per_head = lambda rows, cols: pl.BlockSpec((heads, rows, cols), lambda b, h, t: (h, 0, 0))
    state_spec = pl.BlockSpec((None, heads, RET_QK_DIM, RET_V_DIM), lambda b, h, t: (b, h, 0, 0))
    in_specs = [pl.BlockSpec(memory_space=pltpu.SMEM), tile, tile, tile, tile,
                per_head(block, block), per_head(block, RET_QK_DIM), per_head(block, RET_QK_DIM),
                per_head(1, RET_V_DIM)]
    args = [gl, rq, rk, rv, g1, dec, qd, kd, ret_norm_w.reshape(RET_HEADS, 1, RET_V_DIM)]
    if has_state:
        in_specs.append(state_spec)
        args.append(state0)
    return pl.pallas_call(
        functools.partial(_retention_kernel, n_sub=step // block, block=block, heads=heads,
                          has_state=has_state),
        grid=(batch, RET_HEADS // heads, n_t),
        in_specs=in_specs,
        out_specs=[tile, state_spec],
        out_shape=[jax.ShapeDtypeStruct(rq.shape, BF16),
                   jax.ShapeDtypeStruct((batch, RET_HEADS, RET_QK_DIM, RET_V_DIM), F32)],
        scratch_shapes=[pltpu.VMEM((heads, RET_QK_DIM, RET_V_DIM), F32)],
        compiler_params=pltpu.CompilerParams(
            dimension_semantics=("arbitrary", "arbitrary", "arbitrary"),
            vmem_limit_bytes=VMEM_LIMIT_BYTES),
        name="retention",
    )(*args)


def _suffix_matrix(n):
    idx = jnp.arange(n)
    m = (idx[:, None] > idx[None, :]).astype(BF16)
    return jnp.concatenate([m, m], axis=0)


def _sb_gates(z, mask):
    neg_abs = pltpu.bitcast(pltpu.bitcast(z, jnp.uint32) | jnp.uint32(0x80000000), F32)
    soft = jnp.log2(1.0 + jnp.exp2(neg_abs))
    log_beta = jnp.minimum(z, 0.0) - soft
    u = log_beta - z
    if mask is not None:
        u = jnp.where(mask, u, 0.0)
        log_beta = jnp.where(mask, log_beta, MASKED_LOG_BETA)
    u_hi = u.astype(BF16)
    u_lo = (u - u_hi.astype(F32)).astype(BF16)
    return log_beta, jnp.concatenate([u_hi, u_lo], axis=1), jnp.sum(u, axis=-1, keepdims=True)


def _sb_weights(log_beta, after, carry):
    return jnp.exp2(log_beta + carry + after).astype(BF16)


def _sb_key_run(q, ks, vs, tri2, carry, mask):
    n = len(ks)
    z = [_nt_dot(q, ks[i]) for i in range(n)]
    gates = [_sb_gates(z[i], mask) for i in range(n)]
    after = [_dot(gates[i][1], tri2) for i in range(n)]
    out = None
    for i in range(n):
        o = _dot(_sb_weights(gates[i][0], after[i], carry), vs[i])
        out = o if out is None else out + o
        carry = carry + gates[i][2]
    return out, carry


def _sb_doubled_gates(half_z, visible):
    t = jnp.tanh(half_z)
    if visible is not None:
        t = jnp.where(visible, t, -1.0)
    return 1.0 + t, 1.0 - t


def _sb_run_scan(half_z, visible):
    groups = half_z.shape[0] // SUBLANES
    pieces = [None] * groups
    totals = []
    for first in range(0, groups, SCAN_SEGMENT):
        run = None
        for r in reversed(range(first, first + SCAN_SEGMENT)):
            rows = slice(r * SUBLANES, (r + 1) * SUBLANES)
            beta2, rest2 = _sb_doubled_gates(half_z[rows], None if visible is None else visible[rows])
            pieces[r] = beta2 if run is None else beta2 * run
            run = rest2 if run is None else run * rest2
        totals.append(run * 2.0 ** -SCAN_SEGMENT)
    return pieces, jnp.concatenate(totals, axis=0)


def _value_scale(n):
    place = (jnp.arange(n) // SUBLANES) % SCAN_SEGMENT
    return jnp.exp2((place - SCAN_SEGMENT).astype(F32))


def _suffix_over_sublanes(t):
    s = lax.broadcasted_iota(jnp.int32, t.shape, 0)
    y = t
    for k in (1, 2, 4):
        below = pltpu.roll(y, SUBLANES - k, axis=0)
        y = y * jnp.where(s + k < SUBLANES, below, 1.0)
    exclusive = jnp.where(s + 1 < SUBLANES, pltpu.roll(y, SUBLANES - 1, axis=0), 1.0)
    return exclusive, jnp.broadcast_to(y[0:1, :], t.shape)


def _sb_finish(pieces, totals, v_t, carry):
    segments = totals.shape[0] // SUBLANES
    seg_totals = [totals[j * SUBLANES:(j + 1) * SUBLANES] for j in range(segments)]
    later_segments = [None] * segments
    run_total = None
    for j in reversed(range(segments)):
        later_segments[j] = run_total
        run_total = seg_totals[j] if run_total is None else run_total * seg_totals[j]
    later_runs, total = _suffix_over_sublanes(run_total)
    base = later_runs * carry
    slab = 2 * SUBLANES
    slabs_per_segment = pieces.shape[0] // slab // segments
    w = []
    for j in range(segments):
        factor = base if later_segments[j] is None else base * later_segments[j]
        factor = jnp.concatenate([factor, factor], axis=0).astype(BF16)
        for g in range(j * slabs_per_segment, (j + 1) * slabs_per_segment):
            w.append(pieces[g * slab:(g + 1) * slab] * factor)
    return _dot(v_t, jnp.concatenate(w, axis=0)), carry * total


def _sb_prompt_kernel(q_ref, k_ref, vt_ref, o_ref, acc_ref, carry_ref, z_ref, pieces_ref, totals_ref,
                      *, blk):
    i = pl.program_id(1)
    heads = SB_HEADS
    cols = [slice(hd * SB_HEAD_DIM, (hd + 1) * SB_HEAD_DIM) for hd in range(heads)]
    parked = tuple(range(heads - FINISH_LAG, heads))

    def logits(block, hd):
        start = pl.multiple_of(block * blk, blk)
        return _nt_dot(k_ref[pl.ds(start, blk), cols[hd]], q_ref[:, cols[hd]])

    def finish(pieces, totals, block, hd):
        out, carry = _sb_finish(pieces, totals, vt_ref[block, cols[hd], :], carry_ref[hd])
        acc_ref[hd] += out
        carry_ref[hd] = carry

    def key_block(block, visible, have_parked):
        z_queue = [z_ref[ahead] for ahead in range(LOGITS_AHEAD)]
        scans = {}
        for hd in range(heads):
            ahead = [hd + LOGITS_AHEAD] if hd + LOGITS_AHEAD < heads else []
            if hd == heads - LOGITS_AHEAD - 1:
                ahead += range(heads, heads + LOGITS_AHEAD)
            for nxt in ahead:
                z_queue.append(logits(block, nxt) if nxt < heads
                               else logits(jnp.maximum(block - 1, 0), nxt - heads))
            if hd < len(parked):
                if have_parked:
                    finish(pieces_ref[hd], totals_ref[hd], block + 1, parked[hd])
            else:
                finish(*scans.pop(hd - len(parked)), block, hd - len(parked))
            pieces, totals = _sb_run_scan(z_queue.pop(0), visible)
            scans[hd] = (jnp.concatenate(pieces, axis=0).astype(BF16), totals)
        for slot, hd in enumerate(parked):
            pieces_ref[slot], totals_ref[slot] = scans[hd]
        for ahead in range(LOGITS_AHEAD):
            z_ref[ahead] = z_queue[ahead]

    acc_ref[...] = jnp.zeros_like(acc_ref)
    carry_ref[...] = jnp.ones_like(carry_ref)
    for ahead in range(LOGITS_AHEAD):
        z_ref[ahead] = logits(i, ahead)
    row = lax.broadcasted_iota(jnp.int32, (blk, blk), 0)
    key = (row & (SUBLANES - 1)) * (blk // SUBLANES) + (row >> (SUBLANES.bit_length() - 1))
    key_block(i, key < lax.broadcasted_iota(jnp.int32, (blk, blk), 1), False)

    for p in [1 << b for b in range(SB_KEY_BLOCKS_PER_TRIP.bit_length() - 1)]:
        @pl.when((i & p) != 0)
        def _(p=p):
            top = i - 1 - (i & (p - 1))
            for d in range(p):
                key_block(top - d, None, True)

    def trip(t, _):
        top = i - 1 - (i & (SB_KEY_BLOCKS_PER_TRIP - 1)) - SB_KEY_BLOCKS_PER_TRIP * t
        for d in range(SB_KEY_BLOCKS_PER_TRIP):
            key_block(top - d, None, True)
        return 0

    lax.fori_loop(0, i // SB_KEY_BLOCKS_PER_TRIP, trip, 0)
    for slot, hd in enumerate(parked):
        finish(pieces_ref[slot], totals_ref[slot], 0, hd)
    for hd in range(heads):
        o_ref[:, cols[hd]] = acc_ref[hd].T.astype(o_ref.dtype)


def _sb_prompt(sq, sk_km, sv_t, batch, frames):
    blk = SB_BLOCK
    assert frames % blk == 0
    n_q = frames // blk
    q_tile = pl.BlockSpec((blk, D_MODEL), lambda b, i: (b * n_q + i, 0))
    k_all = pl.BlockSpec((frames, D_MODEL), lambda b, i: (b, 0))
    vt_all = pl.BlockSpec((None, n_q, D_MODEL, blk), lambda b, i: (b, 0, 0, 0))
    return pl.pallas_call(
        functools.partial(_sb_prompt_kernel, blk=blk),
        grid=(batch, n_q),
        in_specs=[q_tile, k_all, vt_all],
        out_specs=q_tile,
        out_shape=jax.ShapeDtypeStruct(sq.shape, BF16),
        scratch_shapes=[pltpu.VMEM((SB_HEADS, SB_HEAD_DIM, blk), F32),
                        pltpu.VMEM((SB_HEADS, SUBLANES, blk), F32),
                        pltpu.VMEM((LOGITS_AHEAD, blk, blk), F32),
                        pltpu.VMEM((FINISH_LAG, blk, blk), BF16),
                        pltpu.VMEM((FINISH_LAG, blk // SCAN_SEGMENT, blk), F32)],
        compiler_params=pltpu.CompilerParams(
            dimension_semantics=("arbitrary", "arbitrary"), vmem_limit_bytes=VMEM_LIMIT_BYTES),
        name="sb_prompt",
    )(sq, sk_km, sv_t.reshape(batch, n_q, D_MODEL, blk))


def _sb_decode_kernel(q_ref, kn_ref, vn_ref, kc_ref, vc_ref, tri_ref, o_ref,
                      qbd_ref, acc_ref, carry_ref, *, new, key_tile, bk):
    kt = pl.program_id(1)
    rows = SB_HEADS * new
    tri2 = tri_ref[...]

    @pl.when(kt == 0)
    def _():
        q = q_ref[...]
        col = lax.broadcasted_iota(jnp.int32, (new, D_MODEL), 1)
        for hd in range(SB_HEADS):
            keep = (col >= hd * SB_HEAD_DIM) & (col < (hd + 1) * SB_HEAD_DIM)
            qbd_ref[hd * new:(hd + 1) * new, :] = jnp.where(keep, q, jnp.zeros_like(q))
        l_q = jnp.concatenate([lax.broadcasted_iota(jnp.int32, (new, new), 0)] * SB_HEADS, axis=0)
        l_k = lax.broadcasted_iota(jnp.int32, (rows, new), 1)
        tri2_new = jnp.concatenate([tri2[:new, :new], tri2[bk:bk + new, :new]], axis=0)
        out, carry = _sb_key_run(qbd_ref[...], [kn_ref[...]], [vn_ref[...]], tri2_new,
                                 jnp.zeros((rows, 1), F32), l_k < l_q)
        acc_ref[...] = out
        carry_ref[...] = carry

    def head_major(ref, sb):
        base = sb * bk * SB_HEADS
        return jnp.concatenate(
            [ref[pl.ds(base + hd, bk, stride=SB_HEADS), :].astype(BF16) for hd in range(SB_HEADS)], axis=1)

    order = list(reversed(range(key_tile // bk)))
    out, carry = _sb_key_run(qbd_ref[...], [head_major(kc_ref, sb) for sb in order],
                             [head_major(vc_ref, sb) for sb in order], tri2, carry_ref[...], None)
    acc_ref[...] += out
    carry_ref[...] = carry

    @pl.when(kt == pl.num_programs(1) - 1)
    def _():
        for hd in range(SB_HEADS):
            cols = slice(hd * SB_HEAD_DIM, (hd + 1) * SB_HEAD_DIM)
            o_ref[:, cols] = acc_ref[hd * new:(hd + 1) * new, cols].astype(o_ref.dtype)


def _sb_decode(sq, sk, sv, cache_k, cache_v, batch, new):
    past = cache_k.shape[1] // SB_HEADS
    key_tile = min(DECODE_KEY_TILE, past)
    bk = min(SB_BLOCK, key_tile)
    assert past % key_tile == 0 and key_tile % bk == 0 and new <= bk
    n_kt = past // key_tile
    rows = SB_HEADS * new
    q_tile = pl.BlockSpec((new, D_MODEL), lambda b, kt: (b, 0))
    cache_tile = pl.BlockSpec((None, key_tile * SB_HEADS, SB_HEAD_DIM),
                              lambda b, kt: (b, n_kt - 1 - kt, 0))
    return pl.pallas_call(
        functools.partial(_sb_decode_kernel, new=new, key_tile=key_tile, bk=bk),
        grid=(batch, n_kt),
        in_specs=[q_tile, q_tile, q_tile, cache_tile, cache_tile, _resident((2 * bk, bk))],
        out_specs=q_tile,
        out_shape=jax.ShapeDtypeStruct(sq.shape, BF16),
        scratch_shapes=[pltpu.VMEM((rows, D_MODEL), BF16), pltpu.VMEM((rows, D_MODEL), F32),
                        pltpu.VMEM((rows, 1), F32)],
        compiler_params=pltpu.CompilerParams(
            dimension_semantics=("arbitrary", "arbitrary"), vmem_limit_bytes=VMEM_LIMIT_BYTES),
        name="sb_decode",
    )(sq, sk, sv, cache_k, cache_v, _suffix_matrix(bk))


def _out_mlp_kernel(x_ref, r_ref, s_ref, g2_ref, wo_ref, nw_ref, wu_ref, wd_ref, y_ref, *, chains):
    n = x_ref.shape[0] // chains
    rows = [slice(c * n, (c + 1) * n) for c in range(chains)]
    mix = [(r_ref[rw, :].astype(F32) + g2_ref[rw, :].astype(F32) * s_ref[rw, :].astype(F32)).astype(BF16)
           for rw in rows]
    x1 = [x_ref[rw, :] + _dot(m, wo_ref[...]) for rw, m in zip(rows, mix)]
    h = [(_rms(v) * nw_ref[...]).astype(BF16) for v in x1]
    up = [jnp.maximum(_dot(v, wu_ref[...]), 0.0) for v in h]
    act = [(u * u).astype(BF16) for u in up]
    for rw, v, a in zip(rows, x1, act):
        y_ref[rw, :] = v + _dot(a, wd_ref[...])


def _out_mlp(x, r, s, g2, w_out_bf, norm_w, w_up_bf, w_down_bf):
    n = x.shape[0]
    chains = OUT_MLP_CHAINS if n % (OUT_MLP_CHAINS * TOKEN_TILE) == 0 else 1
    tm = chains * TOKEN_TILE
    assert n % tm == 0
    tile = pl.BlockSpec((tm, D_MODEL), lambda i: (i, 0))
    return pl.pallas_call(
        functools.partial(_out_mlp_kernel, chains=chains),
        grid=(n // tm,),
        in_specs=[tile, tile, tile, tile, _resident((D_MODEL, D_MODEL)), _resident((1, D_MODEL)),
                  _resident((D_MODEL, D_FF)), _resident((D_FF, D_MODEL))],
        out_specs=tile,
        out_shape=jax.ShapeDtypeStruct((n, D_MODEL), F32),
        compiler_params=pltpu.CompilerParams(
            dimension_semantics=("arbitrary",), vmem_limit_bytes=VMEM_LIMIT_BYTES),
        name="out_mlp",
    )(x, r, s, g2, w_out_bf, norm_w, w_up_bf, w_down_bf)


def _rope_tables(pos):
    half = RET_QK_DIM // 2
    inv_freq = ROPE_BASE ** (-jnp.arange(half, dtype=F32) / half)
    ang = pos.astype(F32)[:, None] * inv_freq[None, :]
    return jnp.cos(ang), jnp.sin(ang)


def kernel(x_prompt, x_sample, cache_sb_k, cache_sb_v, state_ret, norm_mix_w, w_in, ret_norm_w,
           sb_q_norm_w, sb_k_norm_w, w_out, norm_mlp_w, w_up, w_down):
    depth = w_in.shape[0]
    batch, frames, _ = x_prompt.shape
    dec_batch, new, _ = x_sample.shape
    past = cache_sb_k.shape[2]

    cos_p, sin_p = _rope_tables(jnp.arange(frames, dtype=jnp.int32))
    cos_s, sin_s = _rope_tables(past + jnp.arange(new, dtype=jnp.int32))
    reps = max(PROJ_CHAINS * TOKEN_TILE // new, 1)
    cos_s, sin_s = jnp.tile(cos_s, (reps, 1)), jnp.tile(sin_s, (reps, 1))

    xp = x_prompt.reshape(batch * frames, D_MODEL)
    xs = x_sample.reshape(dec_batch * new, D_MODEL)
    outs = [[] for _ in range(6)]
    for l in range(depth):
        w_in_bf = w_in[l].astype(BF16)
        w_out_bf = w_out[l].astype(BF16)
        w_up_bf = w_up[l].astype(BF16)
        w_down_bf = w_down[l].astype(BF16)
        nmix = norm_mix_w[l].reshape(1, D_MODEL)
        nmlp = norm_mlp_w[l].reshape(1, D_MODEL)
        qnw = sb_q_norm_w[l].reshape(1, SB_HEAD_DIM)
        knw = sb_k_norm_w[l].reshape(1, SB_HEAD_DIM)

        rq, rk, rv, g1, sq, sk, skb, sv, svb, g2 = _project(xp, nmix, w_in_bf, cos_p, sin_p, qnw, knw, True)
        r, state_p = _retention(rq, rk, rv, g1, ret_norm_w[l], None, batch, frames)
        o_sb = _sb_prompt(sq, skb, svb, batch, frames)
        xp = _out_mlp(xp, r, o_sb, g2, w_out_bf, nmlp, w_up_bf, w_down_bf)
        outs[0].append(sk.reshape(batch, frames, SB_HEADS, SB_HEAD_DIM))
        outs[1].append(sv.reshape(batch, frames, SB_HEADS, SB_HEAD_DIM))
        outs[2].append(state_p.astype(state_ret.dtype))

        rq, rk, rv, g1, sq, sk, skb, sv, svb, g2 = _project(xs, nmix, w_in_bf, cos_s, sin_s, qnw, knw, False)
        r, state_s = _retention(rq, rk, rv, g1, ret_norm_w[l], state_ret[l].astype(F32), dec_batch, new)
        o_sb = _sb_decode(sq, skb, svb,
                          cache_sb_k[l].reshape(dec_batch, past * SB_HEADS, SB_HEAD_DIM),
                          cache_sb_v[l].reshape(dec_batch, past * SB_HEADS, SB_HEAD_DIM), dec_batch, new)
        xs = _out_mlp(xs, r, o_sb, g2, w_out_bf, nmlp, w_up_bf, w_down_bf)
        outs[3].append(sk.reshape(dec_batch, new, SB_HEADS, SB_HEAD_DIM))
        outs[4].append(sv.reshape(dec_batch, new, SB_HEADS, SB_HEAD_DIM))
        outs[5].append(state_s.astype(state_ret.dtype))

    return (xp.reshape(batch, frames, D_MODEL), xs.reshape(dec_batch, new, D_MODEL),
            jnp.stack(outs[0]), jnp.stack(outs[1]), jnp.stack(outs[2]),
            jnp.stack(outs[3]), jnp.stack(outs[4]), jnp.stack(outs[5]))
```

```python
import functools
import math

import jax
import jax.numpy as jnp
from jax import lax
from jax.experimental import pallas as pl
from jax.experimental.pallas import tpu as pltpu

D_MODEL = 1024
CHUNK = 64
RET_HEADS = 4
RET_QK_DIM = 256
RET_V_DIM = D_MODEL // RET_HEADS
SB_HEADS = 8
SB_HEAD_DIM = D_MODEL // SB_HEADS
D_FF = 4 * D_MODEL
ROPE_BASE = 10000.0
EPS = 1e-6
LOG2_E = math.log2(math.e)
N_GROUPS = 9

F32 = jnp.float32
BF16 = jnp.bfloat16

VMEM_LIMIT_BYTES = 56 * 1024 * 1024
TOKEN_TILE = 256
PROJ_CHAINS = 2
OUT_MLP_CHAINS = 2
RET_BLOCK = 256
RET_STEP = 4096
RET_HEADS_PER_STEP = 2
RET_HEADS_PER_SHORT_STEP = 4
SB_BLOCK = 256
SCAN_SEGMENT = 8
SB_KEY_BLOCKS_PER_TRIP = 4
FINISH_LAG = 2
LOGITS_AHEAD = 2
SUBLANES = 8
SB_PROMPT_QUERY_SCALE = SB_HEAD_DIM ** -0.5 * 0.5
SB_DECODE_QUERY_SCALE = SB_HEAD_DIM ** -0.5 * LOG2_E
MASKED_LOG_BETA = -1e30
DECODE_KEY_TILE = 2048


def _nt_dot(a, b):
    return lax.dot_general(a, b, (((1,), (1,)), ((), ())), preferred_element_type=F32)


def _tn_dot(a, b):
    return lax.dot_general(a, b, (((0,), (0,)), ((), ())), preferred_element_type=F32)


def _dot(a, b):
    return jnp.dot(a, b, preferred_element_type=F32)


def _sigmoid(x):
    return 1.0 / (1.0 + jnp.exp(-x))


def _rms(x):
    return x * lax.rsqrt(jnp.mean(x * x, axis=-1, keepdims=True) + EPS)


def _resident(shape):
    return pl.BlockSpec(shape, lambda *_: (0,) * len(shape), pipeline_mode=pl.Buffered(1))


def _proj_kernel(*refs, key_major):
    if key_major:
        (x_ref, nw_ref, w_ref, cos_ref, sin_ref, qnw_ref, knw_ref, perm_ref, perm_t_ref,
         rq_ref, rk_ref, rv_ref, g1_ref, sq_ref, skf_ref, skb_ref, svf_ref, svb_ref, g2_ref) = refs
    else:
        (x_ref, nw_ref, w_ref, cos_ref, sin_ref, qnw_ref, knw_ref,
         rq_ref, rk_ref, rv_ref, g1_ref, sq_ref, skf_ref, skb_ref, svf_ref, svb_ref, g2_ref) = refs
    chains = x_ref.shape[0] // TOKEN_TILE
    blocks = [slice(c * TOKEN_TILE, (c + 1) * TOKEN_TILE) for c in range(chains)]
    h = [(_rms(x_ref[rw, :]) * nw_ref[...]).astype(BF16) for rw in blocks]

    def group(g):
        return [_dot(hc, w_ref[:, g * D_MODEL:(g + 1) * D_MODEL]) for hc in h]

    half = RET_QK_DIM // 2

    def rotary(p, out_ref, rw, scale):
        cos = cos_ref[rw, :]
        sin = sin_ref[rw, :]
        for hd in range(RET_HEADS):
            lo = hd * RET_QK_DIM
            x1 = p[:, lo:lo + half]
            x2 = p[:, lo + half:lo + 2 * half]
            o1 = x1 * cos - x2 * sin
            o2 = x1 * sin + x2 * cos
            if scale != 1.0:
                o1 = o1 * scale
                o2 = o2 * scale
            out_ref[rw, lo:lo + half] = o1.astype(out_ref.dtype)
            out_ref[rw, lo + half:lo + 2 * half] = o2.astype(out_ref.dtype)

    def head_norm(p, w, scale=1.0):
        ys = []
        for hd in range(SB_HEADS):
            y = _rms(p[:, hd * SB_HEAD_DIM:(hd + 1) * SB_HEAD_DIM]) * w
            ys.append(y if scale == 1.0 else y * scale)
        return jnp.concatenate(ys, axis=1)

    for rw, p in zip(blocks, group(0)):
        rotary(p, rq_ref, rw, 1.0)
    for rw, p in zip(blocks, group(1)):
        rotary(p, rk_ref, rw, RET_QK_DIM ** -0.5)
    for rw, p in zip(blocks, group(2)):
        rv_ref[rw, :] = p.astype(BF16)
    rg = group(3)
    query_scale = SB_PROMPT_QUERY_SCALE if key_major else SB_DECODE_QUERY_SCALE
    for rw, p in zip(blocks, group(4)):
        sq_ref[rw, :] = head_norm(p, qnw_ref[...], query_scale).astype(BF16)
    for c, (rw, p) in enumerate(zip(blocks, group(5))):
        sk = head_norm(p, knw_ref[...])
        skf_ref[rw, :] = sk
        skb_ref[rw, :] = (_dot(perm_ref[...], sk.astype(BF16)) if key_major else sk).astype(BF16)
    for c, (rw, sv) in enumerate(zip(blocks, group(6))):
        svf_ref[rw, :] = sv
        if key_major:
            svb_ref[c] = _tn_dot(sv.astype(BF16), perm_t_ref[...]).astype(BF16)
        else:
            svb_ref[rw, :] = sv.astype(BF16)
    for rw, r, a in zip(blocks, rg, group(7)):
        g1_ref[rw, :] = (r * _sigmoid(r) * _sigmoid(a)).astype(BF16)
    for rw, p in zip(blocks, group(8)):
        g2_ref[rw, :] = _sigmoid(p).astype(BF16)


def _key_order(n):
    row = jnp.arange(n)
    key = (row % SUBLANES) * (n // SUBLANES) + row // SUBLANES
    return (key[:, None] == jnp.arange(n)[None, :]).astype(BF16)


def _project(x, norm_w, w_in_bf, cos_tab, sin_tab, qnw, knw, key_major):
    n = x.shape[0]
    tm = PROJ_CHAINS * TOKEN_TILE
    assert n % tm == 0 and cos_tab.shape[0] % tm == 0
    pos_blocks = cos_tab.shape[0] // tm
    tile = pl.BlockSpec((tm, D_MODEL), lambda i: (i, 0))
    pos_tile = pl.BlockSpec((tm, RET_QK_DIM // 2), lambda i: (i % pos_blocks, 0))
    bf = jax.ShapeDtypeStruct((n, D_MODEL), BF16)
    f32 = jax.ShapeDtypeStruct((n, D_MODEL), F32)
    in_specs = [tile, _resident((1, D_MODEL)), _resident((D_MODEL, N_GROUPS * D_MODEL)),
                pos_tile, pos_tile, _resident((1, SB_HEAD_DIM)), _resident((1, SB_HEAD_DIM))]
    args = [x, norm_w, w_in_bf, cos_tab, sin_tab, qnw, knw]
    out_specs = [tile] * 10
    out_shape = [bf, bf, bf, bf, bf, f32, bf, f32, bf, bf]
    if key_major:
        blk = TOKEN_TILE
        assert blk == SB_BLOCK
        perm = _key_order(blk)
        in_specs += [_resident((blk, blk)), _resident((blk, blk))]
        args += [perm, (perm.T.astype(F32) * _value_scale(blk)[None, :]).astype(BF16)]
        out_specs[8] = pl.BlockSpec((PROJ_CHAINS, D_MODEL, blk), lambda i: (i, 0, 0))
        out_shape[8] = jax.ShapeDtypeStruct((n // blk, D_MODEL, blk), BF16)
    return pl.pallas_call(
        functools.partial(_proj_kernel, key_major=key_major),
        grid=(n // tm,),
        in_specs=in_specs,
        out_specs=out_specs,
        out_shape=out_shape,
        compiler_params=pltpu.CompilerParams(
            dimension_semantics=("arbitrary",), vmem_limit_bytes=VMEM_LIMIT_BYTES),
        name="proj",
    )(*args)


def _retention_kernel(*refs, n_sub, block, heads, has_state):
    if has_state:
        gl_ref, q_ref, k_ref, v_ref, g1_ref, dec_ref, qd_ref, kd_ref, rnw_ref, s0_ref = refs[:10]
        rest = refs[10:]
    else:
        gl_ref, q_ref, k_ref, v_ref, g1_ref, dec_ref, qd_ref, kd_ref, rnw_ref = refs[:9]
        s0_ref = None
        rest = refs[9:]
    r_ref, sout_ref, s_ref = rest
    first_head = pl.program_id(1) * heads
    t = pl.program_id(2)

    @pl.when(t == 0)
    def _():
        if has_state:
            s_ref[...] = s0_ref[...]
        else:
            s_ref[...] = jnp.zeros_like(s_ref)

    hs = range(heads)
    cols = [slice(j * RET_QK_DIM, (j + 1) * RET_QK_DIM) for j in hs]
    for sb in range(n_sub):
        rows = slice(sb * block, (sb + 1) * block)
        q = [q_ref[rows, c] for c in cols]
        k = [k_ref[rows, c] for c in cols]
        v = [v_ref[rows, c] for c in cols]
        s_prev = [s_ref[j] for j in hs]
        scores = [(_nt_dot(q[j], k[j]) * dec_ref[j]).astype(BF16) for j in hs]
        cross = [_dot(q[j], s_prev[j].astype(BF16)) * qd_ref[j] for j in hs]
        k_dec = [(k[j].astype(F32) * kd_ref[j]).astype(BF16) for j in hs]
        for j in hs:
            s_ref[j] = gl_ref[first_head + j] * s_prev[j] + _tn_dot(k_dec[j], v[j])
        for j in hs:
            r = _rms(_dot(scores[j], v[j]) + cross[j]) * rnw_ref[j]
            r_ref[rows, cols[j]] = (r * g1_ref[rows, cols[j]].astype(F32)).astype(BF16)

    @pl.when(t == pl.num_programs(2) - 1)
    def _():
        sout_ref[...] = s_ref[...]


def _retention_tables(block):
    log_gamma = jnp.log1p(-jnp.exp2(-5.0 - jnp.arange(RET_HEADS, dtype=F32)))
    idx = jnp.arange(block, dtype=F32)
    dist = jnp.abs(idx[:, None] - idx[None, :])
    chunk = jnp.arange(block, dtype=jnp.int32) // CHUNK
    visible = chunk[None, :] <= chunk[:, None]
    dec = jnp.where(visible[None], jnp.exp(log_gamma[:, None, None] * dist[None]), 0.0)
    qd = jnp.exp(log_gamma[:, None] * (idx[None, :] + 1.0))
    kd = jnp.exp(log_gamma[:, None] * (block - 1.0 - idx[None, :]))
    widen = lambda a: jnp.broadcast_to(a[:, :, None], (RET_HEADS, block, RET_QK_DIM))
    gl = jnp.exp(log_gamma * block)
    return gl, dec, widen(qd), widen(kd)


def _retention(rq, rk, rv, g1, ret_norm_w, state0, batch, frames, state_layer=0):
    block = min(RET_BLOCK, frames)
    step = min(RET_STEP, frames)
    assert frames % step == 0 and step % block == 0
    assert block % CHUNK == 0 or block < CHUNK
    n_t = frames // step
    gl, dec, qd, kd = _retention_tables(block)
    has_state = state0 is not None
    heads = RET_HEADS_PER_SHORT_STEP if step == block else RET_HEADS_PER_STEP
    assert RET_HEADS % heads == 0
    tile = pl.BlockSpec((step, heads * RET_QK_DIM), lambda b, h, t: (b * n_t + t, h))
    per_head = lambda rows, cols: pl.BlockSpec((heads, rows, cols), lambda b, h, t: (h, 0, 0))
    state_spec = pl.BlockSpec((None, heads, RET_QK_DIM, RET_V_DIM), lambda b, h, t: (b, h, 0, 0))
    in_specs = [pl.BlockSpec(memory_space=pltpu.SMEM), tile, tile, tile, tile,
                per_head(block, block), per_head(block, RET_QK_DIM), per_head(block, RET_QK_DIM),
                per_head(1, RET_V_DIM)]
    args = [gl, rq, rk, rv, g1, dec, qd, kd, ret_norm_w.reshape(RET_HEADS, 1, RET_V_DIM)]
    if has_state:
        first = state_layer * batch
        in_specs.append(pl.BlockSpec((None, heads, RET_QK_DIM, RET_V_DIM),
                                     lambda b, h, t: (first + b, h, 0, 0)))
        args.append(state0.reshape(-1, RET_HEADS, RET_QK_DIM, RET_V_DIM))
    return pl.pallas_call(
        functools.partial(_retention_kernel, n_sub=step // block, block=block, heads=heads,
                          has_state=has_state),
        grid=(batch, RET_HEADS // heads, n_t),
        in_specs=in_specs,
        out_specs=[tile, state_spec],
        out_shape=[jax.ShapeDtypeStruct(rq.shape, BF16),
                   jax.ShapeDtypeStruct((batch, RET_HEADS, RET_QK_DIM, RET_V_DIM), F32)],
        scratch_shapes=[pltpu.VMEM((heads, RET_QK_DIM, RET_V_DIM), F32)],
        compiler_params=pltpu.CompilerParams(
            dimension_semantics=("arbitrary", "arbitrary", "arbitrary"),
            vmem_limit_bytes=VMEM_LIMIT_BYTES),
        name="retention",
    )(*args)


def _suffix_matrix(n):
    idx = jnp.arange(n)
    m = (idx[:, None] > idx[None, :]).astype(BF16)
    return jnp.concatenate([m, m], axis=0)


def _sb_gates(z, mask):
    neg_abs = pltpu.bitcast(pltpu.bitcast(z, jnp.uint32) | jnp.uint32(0x80000000), F32)
    soft = jnp.log2(1.0 + jnp.exp2(neg_abs))
    log_beta = jnp.minimum(z, 0.0) - soft
    u = log_beta - z
    if mask is not None:
        u = jnp.where(mask, u, 0.0)
        log_beta = jnp.where(mask, log_beta, MASKED_LOG_BETA)
    u_hi = u.astype(BF16)
    u_lo = (u - u_hi.astype(F32)).astype(BF16)
    return log_beta, jnp.concatenate([u_hi, u_lo], axis=1), jnp.sum(u, axis=-1, keepdims=True)


def _sb_weights(log_beta, after, carry):
    return jnp.exp2(log_beta + carry + after).astype(BF16)


def _sb_key_run(q, ks, vs, tri2, carry, mask):
    n = len(ks)
    z = [_nt_dot(q, ks[i]) for i in range(n)]
    gates = [_sb_gates(z[i], mask) for i in range(n)]
    after = [_dot(gates[i][1], tri2) for i in range(n)]
    out = None
    for i in range(n):
        o = _dot(_sb_weights(gates[i][0], after[i], carry), vs[i])
        out = o if out is None else out + o
        carry = carry + gates[i][2]
    return out, carry


def _sb_doubled_gates(half_z, visible):
    t = jnp.tanh(half_z)
    if visible is not None:
        t = jnp.where(visible, t, -1.0)
    return 1.0 + t, 1.0 - t


def _sb_run_scan(half_z, visible):
    groups = half_z.shape[0] // SUBLANES
    pieces = [None] * groups
    totals = []
    for first in range(0, groups, SCAN_SEGMENT):
        run = None
        for r in reversed(range(first, first + SCAN_SEGMENT)):
            rows = slice(r * SUBLANES, (r + 1) * SUBLANES)
            beta2, rest2 = _sb_doubled_gates(half_z[rows], None if visible is None else visible[rows])
            pieces[r] = beta2 if run is None else beta2 * run
            run = rest2 if run is None else run * rest2
        totals.append(run * 2.0 ** -SCAN_SEGMENT)
    return pieces, jnp.concatenate(totals, axis=0)


def _value_scale(n):
    place = (jnp.arange(n) // SUBLANES) % SCAN_SEGMENT
    return jnp.exp2((place - SCAN_SEGMENT).astype(F32))


def _suffix_over_sublanes(t):
    s = lax.broadcasted_iota(jnp.int32, t.shape, 0)
    y = t
    for k in (1, 2, 4):
        below = pltpu.roll(y, SUBLANES - k, axis=0)
        y = y * jnp.where(s + k < SUBLANES, below, 1.0)
    exclusive = jnp.where(s + 1 < SUBLANES, pltpu.roll(y, SUBLANES - 1, axis=0), 1.0)
    return exclusive, jnp.broadcast_to(y[0:1, :], t.shape)


def _sb_finish(pieces, totals, v_t, carry):
    segments = totals.shape[0] // SUBLANES
    seg_totals = [totals[j * SUBLANES:(j + 1) * SUBLANES] for j in range(segments)]
    later_segments = [None] * segments
    run_total = None
    for j in reversed(range(segments)):
        later_segments[j] = run_total
        run_total = seg_totals[j] if run_total is None else run_total * seg_totals[j]
    later_runs, total = _suffix_over_sublanes(run_total)
    base = later_runs * carry
    slab = 2 * SUBLANES
    slabs_per_segment = pieces.shape[0] // slab // segments
    w = []
    for j in range(segments):
        factor = base if later_segments[j] is None else base * later_segments[j]
        factor = jnp.concatenate([factor, factor], axis=0).astype(BF16)
        for g in range(j * slabs_per_segment, (j + 1) * slabs_per_segment):
            w.append(pieces[g * slab:(g + 1) * slab] * factor)
    return _dot(v_t, jnp.concatenate(w, axis=0)), carry * total


def _sb_prompt_kernel(q_ref, k_ref, vt_ref, o_ref, acc_ref, carry_ref, z_ref, pieces_ref, totals_ref,
                      *, blk):
    i = pl.program_id(1)
    heads = SB_HEADS
    cols = [slice(hd * SB_HEAD_DIM, (hd + 1) * SB_HEAD_DIM) for hd in range(heads)]
    parked = tuple(range(heads - FINISH_LAG, heads))

    def logits(block, hd):
        start = pl.multiple_of(block * blk, blk)
        return _nt_dot(k_ref[pl.ds(start, blk), cols[hd]], q_ref[:, cols[hd]])

    def finish(pieces, totals, block, hd):
        out, carry = _sb_finish(pieces, totals, vt_ref[block, cols[hd], :], carry_ref[hd])
        acc_ref[hd] += out
        carry_ref[hd] = carry

    def key_block(block, visible, have_parked):
        z_queue = [z_ref[ahead] for ahead in range(LOGITS_AHEAD)]
        scans = {}
        for hd in range(heads):
            ahead = [hd + LOGITS_AHEAD] if hd + LOGITS_AHEAD < heads else []
            if hd == heads - LOGITS_AHEAD - 1:
                ahead += range(heads, heads + LOGITS_AHEAD)
            for nxt in ahead:
                z_queue.append(logits(block, nxt) if nxt < heads
                               else logits(jnp.maximum(block - 1, 0), nxt - heads))
            if hd < len(parked):
                if have_parked:
                    finish(pieces_ref[hd], totals_ref[hd], block + 1, parked[hd])
            else:
                finish(*scans.pop(hd - len(parked)), block, hd - len(parked))
            pieces, totals = _sb_run_scan(z_queue.pop(0), visible)
            scans[hd] = (jnp.concatenate(pieces, axis=0).astype(BF16), totals)
        for slot, hd in enumerate(parked):
            pieces_ref[slot], totals_ref[slot] = scans[hd]
        for ahead in range(LOGITS_AHEAD):
            z_ref[ahead] = z_queue[ahead]

    acc_ref[...] = jnp.zeros_like(acc_ref)
    carry_ref[...] = jnp.ones_like(carry_ref)
    for ahead in range(LOGITS_AHEAD):
        z_ref[ahead] = logits(i, ahead)
    row = lax.broadcasted_iota(jnp.int32, (blk, blk), 0)
    key = (row & (SUBLANES - 1)) * (blk // SUBLANES) + (row >> (SUBLANES.bit_length() - 1))
    key_block(i, key < lax.broadcasted_iota(jnp.int32, (blk, blk), 1), False)

    for p in [1 << b for b in range(SB_KEY_BLOCKS_PER_TRIP.bit_length() - 1)]:
        @pl.when((i & p) != 0)
        def _(p=p):
            top = i - 1 - (i & (p - 1))
            for d in range(p):
                key_block(top - d, None, True)

    def trip(t, _):
        top = i - 1 - (i & (SB_KEY_BLOCKS_PER_TRIP - 1)) - SB_KEY_BLOCKS_PER_TRIP * t
        for d in range(SB_KEY_BLOCKS_PER_TRIP):
            key_block(top - d, None, True)
        return 0

    lax.fori_loop(0, i // SB_KEY_BLOCKS_PER_TRIP, trip, 0)
    for slot, hd in enumerate(parked):
        finish(pieces_ref[slot], totals_ref[slot], 0, hd)
    for hd in range(heads):
        o_ref[:, cols[hd]] = acc_ref[hd].T.astype(o_ref.dtype)


def _sb_prompt(sq, sk_km, sv_t, batch, frames):
    blk = SB_BLOCK
    assert frames % blk == 0
    n_q = frames // blk
    q_tile = pl.BlockSpec((blk, D_MODEL), lambda b, i: (b * n_q + i, 0))
    k_all = pl.BlockSpec((frames, D_MODEL), lambda b, i: (b, 0))
    vt_all = pl.BlockSpec((None, n_q, D_MODEL, blk), lambda b, i: (b, 0, 0, 0))
    return pl.pallas_call(
        functools.partial(_sb_prompt_kernel, blk=blk),
        grid=(batch, n_q),
        in_specs=[q_tile, k_all, vt_all],
        out_specs=q_tile,
        out_shape=jax.ShapeDtypeStruct(sq.shape, BF16),
        scratch_shapes=[pltpu.VMEM((SB_HEADS, SB_HEAD_DIM, blk), F32),
                        pltpu.VMEM((SB_HEADS, SUBLANES, blk), F32),
                        pltpu.VMEM((LOGITS_AHEAD, blk, blk), F32),
                        pltpu.VMEM((FINISH_LAG, blk, blk), BF16),
                        pltpu.VMEM((FINISH_LAG, blk // SCAN_SEGMENT, blk), F32)],
        compiler_params=pltpu.CompilerParams(
            dimension_semantics=("arbitrary", "arbitrary"), vmem_limit_bytes=VMEM_LIMIT_BYTES),
        name="sb_prompt",
    )(sq, sk_km, sv_t.reshape(batch, n_q, D_MODEL, blk))


def _sb_decode_kernel(q_ref, kn_ref, vn_ref, kc_ref, vc_ref, tri_ref, o_ref,
                      qbd_ref, acc_ref, carry_ref, *, new, key_tile, bk):
    kt = pl.program_id(1)
    rows = SB_HEADS * new
    tri2 = tri_ref[...]

    @pl.when(kt == 0)
    def _():
        q = q_ref[...]
        col = lax.broadcasted_iota(jnp.int32, (new, D_MODEL), 1)
        for hd in range(SB_HEADS):
            keep = (col >= hd * SB_HEAD_DIM) & (col < (hd + 1) * SB_HEAD_DIM)
            qbd_ref[hd * new:(hd + 1) * new, :] = jnp.where(keep, q, jnp.zeros_like(q))
        l_q = jnp.concatenate([lax.broadcasted_iota(jnp.int32, (new, new), 0)] * SB_HEADS, axis=0)
        l_k = lax.broadcasted_iota(jnp.int32, (rows, new), 1)
        tri2_new = jnp.concatenate([tri2[:new, :new], tri2[bk:bk + new, :new]], axis=0)
        out, carry = _sb_key_run(qbd_ref[...], [kn_ref[...]], [vn_ref[...]], tri2_new,
                                 jnp.zeros((rows, 1), F32), l_k < l_q)
        acc_ref[...] = out
        carry_ref[...] = carry

    def head_major(ref, sb):
        base = sb * bk * SB_HEADS
        return jnp.concatenate(
            [ref[pl.ds(base + hd, bk, stride=SB_HEADS), :].astype(BF16) for hd in range(SB_HEADS)], axis=1)

    order = list(reversed(range(key_tile // bk)))
    out, carry = _sb_key_run(qbd_ref[...], [head_major(kc_ref, sb) for sb in order],
                             [head_major(vc_ref, sb) for sb in order], tri2, carry_ref[...], None)
    acc_ref[...] += out
    carry_ref[...] = carry

    @pl.when(kt == pl.num_programs(1) - 1)
    def _():
        for hd in range(SB_HEADS):
            cols = slice(hd * SB_HEAD_DIM, (hd + 1) * SB_HEAD_DIM)
            o_ref[:, cols] = acc_ref[hd * new:(hd + 1) * new, cols].astype(o_ref.dtype)


def _sb_decode(sq, sk, sv, cache_k, cache_v, batch, new):
    past = cache_k.shape[1] // SB_HEADS
    key_tile = min(DECODE_KEY_TILE, past)
    bk = min(SB_BLOCK, key_tile)
    assert past % key_tile == 0 and key_tile % bk == 0 and new <= bk
    n_kt = past // key_tile
    rows = SB_HEADS * new
    q_tile = pl.BlockSpec((new, D_MODEL), lambda b, kt: (b, 0))
    cache_tile = pl.BlockSpec((None, key_tile * SB_HEADS, SB_HEAD_DIM),
                              lambda b, kt: (b, n_kt - 1 - kt, 0))
    return pl.pallas_call(
        functools.partial(_sb_decode_kernel, new=new, key_tile=key_tile, bk=bk),
        grid=(batch, n_kt),
        in_specs=[q_tile, q_tile, q_tile, cache_tile, cache_tile, _resident((2 * bk, bk))],
        out_specs=q_tile,
        out_shape=jax.ShapeDtypeStruct(sq.shape, BF16),
        scratch_shapes=[pltpu.VMEM((rows, D_MODEL), BF16), pltpu.VMEM((rows, D_MODEL), F32),
                        pltpu.VMEM((rows, 1), F32)],
        compiler_params=pltpu.CompilerParams(
            dimension_semantics=("arbitrary", "arbitrary"), vmem_limit_bytes=VMEM_LIMIT_BYTES),
        name="sb_decode",
    )(sq, sk, sv, cache_k, cache_v, _suffix_matrix(bk))


def _out_mlp_kernel(x_ref, r_ref, s_ref, g2_ref, wo_ref, nw_ref, wu_ref, wd_ref, y_ref, *, chains):
    n = x_ref.shape[0] // chains
    rows = [slice(c * n, (c + 1) * n) for c in range(chains)]
    mix = [(r_ref[rw, :].astype(F32) + g2_ref[rw, :].astype(F32) * s_ref[rw, :].astype(F32)).astype(BF16)
           for rw in rows]
    x1 = [x_ref[rw, :] + _dot(m, wo_ref[...]) for rw, m in zip(rows, mix)]
    h = [(_rms(v) * nw_ref[...]).astype(BF16) for v in x1]
    up = [jnp.maximum(_dot(v, wu_ref[...]), 0.0) for v in h]
    act = [(u * u).astype(BF16) for u in up]
    for rw, v, a in zip(rows, x1, act):
        y_ref[rw, :] = v + _dot(a, wd_ref[...])


def _out_mlp(x, r, s, g2, w_out_bf, norm_w, w_up_bf, w_down_bf):
    n = x.shape[0]
    chains = OUT_MLP_CHAINS if n % (OUT_MLP_CHAINS * TOKEN_TILE) == 0 else 1
    tm = chains * TOKEN_TILE
    assert n % tm == 0
    tile = pl.BlockSpec((tm, D_MODEL), lambda i: (i, 0))
    return pl.pallas_call(
        functools.partial(_out_mlp_kernel, chains=chains),
        grid=(n // tm,),
        in_specs=[tile, tile, tile, tile, _resident((D_MODEL, D_MODEL)), _resident((1, D_MODEL)),
                  _resident((D_MODEL, D_FF)), _resident((D_FF, D_MODEL))],
        out_specs=tile,
        out_shape=jax.ShapeDtypeStruct((n, D_MODEL), F32),
        compiler_params=pltpu.CompilerParams(
            dimension_semantics=("arbitrary",), vmem_limit_bytes=VMEM_LIMIT_BYTES),
        name="out_mlp",
    )(x, r, s, g2, w_out_bf, norm_w, w_up_bf, w_down_bf)


def _rope_tables(pos):
    half = RET_QK_DIM // 2
    inv_freq = ROPE_BASE ** (-jnp.arange(half, dtype=F32) / half)
    ang = pos.astype(F32)[:, None] * inv_freq[None, :]
    return jnp.cos(ang), jnp.sin(ang)


def kernel(x_prompt, x_sample, cache_sb_k, cache_sb_v, state_ret, norm_mix_w, w_in, ret_norm_w,
           sb_q_norm_w, sb_k_norm_w, w_out, norm_mlp_w, w_up, w_down):
    depth = w_in.shape[0]
    batch, frames, _ = x_prompt.shape
    dec_batch, new, _ = x_sample.shape
    past = cache_sb_k.shape[2]

    cos_p, sin_p = _rope_tables(jnp.arange(frames, dtype=jnp.int32))
    cos_s, sin_s = _rope_tables(past + jnp.arange(new, dtype=jnp.int32))
    reps = max(PROJ_CHAINS * TOKEN_TILE // new, 1)
    cos_s, sin_s = jnp.tile(cos_s, (reps, 1)), jnp.tile(sin_s, (reps, 1))

    xp = x_prompt.reshape(batch * frames, D_MODEL)
    xs = x_sample.reshape(dec_batch * new, D_MODEL)
    outs = [[] for _ in range(6)]
    for l in range(depth):
        w_in_bf = w_in[l].astype(BF16)
        w_out_bf = w_out[l].astype(BF16)
        w_up_bf = w_up[l].astype(BF16)
        w_down_bf = w_down[l].astype(BF16)
        nmix = norm_mix_w[l].reshape(1, D_MODEL)
        nmlp = norm_mlp_w[l].reshape(1, D_MODEL)
        qnw = sb_q_norm_w[l].reshape(1, SB_HEAD_DIM)
        knw = sb_k_norm_w[l].reshape(1, SB_HEAD_DIM)

        rq, rk, rv, g1, sq, sk, skb, sv, svb, g2 = _project(xp, nmix, w_in_bf, cos_p, sin_p, qnw, knw, True)
        r, state_p = _retention(rq, rk, rv, g1, ret_norm_w[l], None, batch, frames)
        o_sb = _sb_prompt(sq, skb, svb, batch, frames)
        xp = _out_mlp(xp, r, o_sb, g2, w_out_bf, nmlp, w_up_bf, w_down_bf)
        outs[0].append(sk.reshape(batch, frames, SB_HEADS, SB_HEAD_DIM))
        outs[1].append(sv.reshape(batch, frames, SB_HEADS, SB_HEAD_DIM))
        outs[2].append(state_p.astype(state_ret.dtype))

        rq, rk, rv, g1, sq, sk, skb, sv, svb, g2 = _project(xs, nmix, w_in_bf, cos_s, sin_s, qnw, knw, False)
        r, state_s = _retention(rq, rk, rv, g1, ret_norm_w[l], state_ret.astype(F32), dec_batch, new, l)
        o_sb = _sb_decode(sq, skb, svb,
                          cache_sb_k[l].reshape(dec_batch, past * SB_HEADS, SB_HEAD_DIM),
                          cache_sb_v[l].reshape(dec_batch, past * SB_HEADS, SB_HEAD_DIM), dec_batch, new)
        xs = _out_mlp(xs, r, o_sb, g2, w_out_bf, nmlp, w_up_bf, w_down_bf)
        outs[3].append(sk.reshape(dec_batch, new, SB_HEADS, SB_HEAD_DIM))
        outs[4].append(sv.reshape(dec_batch, new, SB_HEADS, SB_HEAD_DIM))
        outs[5].append(state_s.astype(state_ret.dtype))

    layers = lambda per_layer: per_layer[0][None] if depth == 1 else jnp.stack(per_layer)
    return (xp.reshape(batch, frames, D_MODEL), xs.reshape(dec_batch, new, D_MODEL),
            *(layers(o) for o in outs))
```
